```python
import math
import jax, jax.numpy as jnp
from jax import lax
import numpy as np

D_MODEL = 4096
BATCH = 4
SEQ = 2048
DEPTH = 2

DN_ALPHA = (2 * DEPTH) ** 0.25
DN_BETA = (8 * DEPTH) ** -0.25
LN_EPS = 1e-5

D_SSM = D_MODEL
SSM_HEAD_DIM = 64
SSM_HEADS = D_SSM // SSM_HEAD_DIM
SSM_GROUPS = 8
SSM_STATE = 128
SSM_CHUNK = 128
CONV_K = 4
GN = SSM_GROUPS * SSM_STATE
CONV_CH = D_SSM + 2 * GN
POOL_WINDOWS = (2, 4, 8, 16)
N_POOL = len(POOL_WINDOWS)
D_POOL = D_MODEL
POOL_GW = D_POOL // N_POOL
N_IN = D_SSM + CONV_CH + SSM_HEADS + D_POOL
D_MIX = D_SSM + D_POOL

MLA_HEADS = 32
Q_LORA = 1024
KV_LORA = 512
QK_NOPE = 128
QK_ROPE = 64
V_DIM = 128
ROPE_THETA = 10000.0
Q_BLOCK = 128
RMS_EPS = 1e-6

N_EXPERTS = 32
TOP_K = 4
D_EXPERT = 512
SWIGLU_LIMIT = 7.0
SWIGLU_ALPHA = 1.702
MOE_BLOCK = 128

N_EVEN = (DEPTH + 1) // 2
N_ODD = DEPTH // 2

kernel_name = 'hybrid_ssd_pool_mla_moe_deepnorm_adaln'


def layer_norm(x, w, b):
    xf = x.astype(jnp.float32)
    mu = jnp.mean(xf, axis=-1, keepdims=True)
    var = jnp.mean(jnp.square(xf - mu), axis=-1, keepdims=True)
    return ((xf - mu) * lax.rsqrt(var + LN_EPS) * w + b).astype(x.dtype)


def rms_norm(x, w, eps):
    xf = x.astype(jnp.float32)
    return (xf * lax.rsqrt(jnp.mean(jnp.square(xf), -1, keepdims=True) + eps) * w).astype(x.dtype)


def rotary(x, pos):
    half = x.shape[-1] // 2
    inv_freq = ROPE_THETA ** (-jnp.arange(half, dtype=jnp.float32) * 2.0 / x.shape[-1])
    ang = pos.astype(jnp.float32)[:, None] * inv_freq[None, :]
    cos = jnp.cos(ang)[:, None, :]
    sin = jnp.sin(ang)[:, None, :]
    xf = x.astype(jnp.float32)
    x1, x2 = xf[..., 0::2], xf[..., 1::2]
    out = jnp.stack([x1 * cos - x2 * sin, x1 * sin + x2 * cos], axis=-1).reshape(x.shape)
    return out.astype(x.dtype)


def causal_dwconv(u, w, bias):
    s = u.shape[1]
    up = jnp.pad(u, ((0, 0), (CONV_K - 1, 0), (0, 0)))
    return sum(w[k] * up[:, k:k + s] for k in range(CONV_K)) + bias


def ssd_chunked(xs, dt, a, bm, cm):
    b, s, g, r, p = xs.shape
    n = bm.shape[-1]
    nc = s // SSM_CHUNK
    L = SSM_CHUNK
    xdt = (xs * dt[..., None]).reshape(b, nc, L, g, r, p)
    la = (dt * a).reshape(b, nc, L, g, r)
    bm = bm.reshape(b, nc, L, g, n)
    cm = cm.reshape(b, nc, L, g, n)
    cum = jnp.cumsum(la, axis=2)
    causal = jnp.tril(jnp.ones((L, L), bool))[None, None, :, :, None, None]
    seg = cum[:, :, :, None] - cum[:, :, None, :]
    decay = jnp.exp(jnp.where(causal, seg, -jnp.inf))
    cb = jnp.einsum('bclgn,bcsgn->bclsg', cm, bm)
    y_diag = jnp.einsum('bclsgr,bcsgrp->bclgrp', cb[..., None] * decay, xdt)
    to_end = jnp.exp(cum[:, :, -1:] - cum)
    states = jnp.einsum('bclgn,bclgrp->bcgrpn', bm, xdt * to_end[..., None])
    chunk_decay = jnp.exp(cum[:, :, -1])

    def carry_state(h, inp):
        st, dec = inp
        return h * dec[..., None, None] + st, h

    h0 = jnp.zeros((b, g, r, p, n), xdt.dtype)
    _, h_in = lax.scan(carry_state, h0, (jnp.moveaxis(states, 1, 0), jnp.moveaxis(chunk_decay, 1, 0)))
    h_in = jnp.moveaxis(h_in, 0, 1)
    y_off = jnp.einsum('bclgn,bcgrpn->bclgrp', cm, h_in) * jnp.exp(cum)[..., None]
    return (y_diag + y_off).reshape(b, s, g, r, p)


def gated_group_rmsnorm(y, z, w):
    b, s, d = y.shape
    gz = (y * jax.nn.silu(z.astype(jnp.float32))).reshape(b, s, SSM_GROUPS, -1)
    gz = gz * lax.rsqrt(jnp.mean(jnp.square(gz), -1, keepdims=True) + LN_EPS)
    return gz.reshape(b, s, d) * w


def multiscale_pool(u, pool_w, pool_scale):
    b, s, _ = u.shape
    uf = u.astype(jnp.float32)
    cs = jnp.cumsum(uf, axis=1)
    t1 = jnp.arange(1, s + 1, dtype=jnp.float32)[None, :, None]
    groups = []
    for g, win in enumerate(POOL_WINDOWS):
        cg = cs[..., g * POOL_GW:(g + 1) * POOL_GW]
        lag = jnp.pad(cg, ((0, 0), (win, 0), (0, 0)))[:, :s]
        groups.append((cg - lag) / jnp.minimum(t1, float(win)))
    diff = (jnp.concatenate(groups, -1) - uf).astype(u.dtype).reshape(b, s, N_POOL, POOL_GW)
    y = jnp.einsum('bsgc,gcd->bsgd', diff, pool_w).reshape(b, s, D_POOL)
    return y * pool_scale


def ssd_pool_mixer(h, in_proj, conv_w, conv_b, dt_bias, a_log, d_skip, norm_w, pool_w, pool_scale, out_proj):
    b, s, _ = h.shape
    proj = h @ in_proj
    o1 = D_SSM
    o2 = o1 + CONV_CH
    o3 = o2 + SSM_HEADS
    z, xbc, dt, u = proj[..., :o1], proj[..., o1:o2], proj[..., o2:o3], proj[..., o3:]
    xbc = jax.nn.silu(causal_dwconv(xbc, conv_w, conv_b)).astype(jnp.float32)
    r = SSM_HEADS // SSM_GROUPS
    xs = xbc[..., :D_SSM].reshape(b, s, SSM_GROUPS, r, SSM_HEAD_DIM)
    bm = xbc[..., D_SSM:D_SSM + GN].reshape(b, s, SSM_GROUPS, SSM_STATE)
    cm = xbc[..., D_SSM + GN:].reshape(b, s, SSM_GROUPS, SSM_STATE)
    dt = jax.nn.softplus(dt.astype(jnp.float32) + dt_bias.astype(jnp.float32)).reshape(b, s, SSM_GROUPS, r)
    a = -jnp.exp(a_log.astype(jnp.float32)).reshape(SSM_GROUPS, r)
    y = ssd_chunked(xs, dt, a, bm, cm) + d_skip.astype(jnp.float32).reshape(SSM_GROUPS, r)[..., None] * xs
    y = gated_group_rmsnorm(y.reshape(b, s, D_SSM), z, norm_w).astype(h.dtype)
    y_pool = multiscale_pool(u, pool_w, pool_scale).astype(h.dtype)
    return jnp.concatenate([y, y_pool], axis=-1) @ out_proj


def mla_mixer(h, wq_a, q_norm_w, wq_b, wkv_a, kv_norm_w, wkv_b, wo, pos):
    b, s, _ = h.shape
    q = (rms_norm(h @ wq_a, q_norm_w, RMS_EPS) @ wq_b).reshape(b, s, MLA_HEADS, QK_NOPE + QK_ROPE)
    q_nope, q_pe = q[..., :QK_NOPE], rotary(q[..., QK_NOPE:], pos)
    kv = h @ wkv_a
    kv_c = rms_norm(kv[..., :KV_LORA], kv_norm_w, RMS_EPS)
    k_pe = rotary(kv[..., KV_LORA:][:, :, None, :], pos)[:, :, 0]
    kvb = (kv_c @ wkv_b).reshape(b, s, MLA_HEADS, QK_NOPE + V_DIM)
    k_nope, v = kvb[..., :QK_NOPE], kvb[..., QK_NOPE:]
    scale = (QK_NOPE + QK_ROPE) ** -0.5
    nblk = s // Q_BLOCK
    qn = jnp.moveaxis(q_nope.reshape(b, nblk, Q_BLOCK, MLA_HEADS, QK_NOPE), 1, 0)
    qp = jnp.moveaxis(q_pe.reshape(b, nblk, Q_BLOCK, MLA_HEADS, QK_ROPE), 1, 0)
    kpos = jnp.arange(s)

    def attend_block(args):
        qn_b, qp_b, start = args
        sc = jnp.einsum('bqhd,bkhd->bhqk', qn_b, k_nope) + jnp.einsum('bqhr,bkr->bhqk', qp_b, k_pe)
        sc = sc.astype(jnp.float32) * scale
        qpos = start + jnp.arange(Q_BLOCK)
        sc = jnp.where(qpos[:, None] >= kpos[None, :], sc, -jnp.inf)
        pr = jax.nn.softmax(sc, axis=-1).astype(v.dtype)
        return jnp.einsum('bhqk,bkhd->bqhd', pr, v)

    o = lax.map(attend_block, (qn, qp, jnp.arange(nblk) * Q_BLOCK))
    o = jnp.moveaxis(o, 0, 1).reshape(b, s, MLA_HEADS * V_DIM)
    return o @ wo


def moe_ffn(h, router_w, router_b, w_gu, b_gu, w_down, b_down):
    b, s, d = h.shape
    n_tok = b * s
    t = h.reshape(n_tok, d)
    logits = (t @ router_w + router_b).astype(jnp.float32)
    top_v, top_e = lax.top_k(logits, TOP_K)
    top_p = jax.nn.softmax(top_v, axis=-1)
    n_asg = n_tok * TOP_K
    flat_e = top_e.reshape(-1)
    flat_tok = jnp.arange(n_asg) // TOP_K
    flat_p = top_p.reshape(-1)
    order = jnp.argsort(flat_e)
    se, stok, sp = flat_e[order], flat_tok[order], flat_p[order]
    counts = jnp.bincount(flat_e, length=N_EXPERTS)
    nblk_e = (counts + MOE_BLOCK - 1) // MOE_BLOCK
    blk_end = jnp.cumsum(nblk_e)
    row_start = (blk_end - nblk_e) * MOE_BLOCK
    seg_start = jnp.cumsum(counts) - counts
    dest = row_start[se] + jnp.arange(n_asg) - seg_start[se]
    n_blocks = -(-(n_asg + N_EXPERTS * (MOE_BLOCK - 1)) // MOE_BLOCK)
    n_rows = n_blocks * MOE_BLOCK
    row_tok = jnp.zeros((n_rows,), jnp.int32).at[dest].set(stok)
    row_p = jnp.zeros((n_rows,), jnp.float32).at[dest].set(sp)
    blk_expert = jnp.minimum(jnp.searchsorted(blk_end, jnp.arange(n_blocks), side='right'), N_EXPERTS - 1)

    def expert_block(args):
        idx, pe, e = args
        xe = t[idx]
        gu = xe @ w_gu[e] + b_gu[e]
        gate = jnp.minimum(gu[:, 0::2], SWIGLU_LIMIT)
        up = jnp.clip(gu[:, 1::2], -SWIGLU_LIMIT, SWIGLU_LIMIT)
        glu = gate * jax.nn.sigmoid(SWIGLU_ALPHA * gate)
        out = ((up + 1.0) * glu) @ w_down[e] + b_down[e]
        return out.astype(jnp.float32) * pe[:, None]

    yb = lax.map(expert_block, (row_tok.reshape(n_blocks, MOE_BLOCK), row_p.reshape(n_blocks, MOE_BLOCK), blk_expert))
    y = jnp.zeros((n_tok, d), jnp.float32).at[row_tok].add(yb.reshape(n_rows, d))
    return y.astype(h.dtype).reshape(b, s, d)


def setup_inputs(seed: int = 0) -> dict:
    key = jax.random.key(seed)
    ks = iter(jax.random.split(key, 32))
    f32 = jnp.float32
    D = D_MODEL

    def nrm(shape, scale):
        return jax.random.normal(next(ks), shape, f32) * scale

    x = nrm((BATCH, SEQ, D), 1.0)
    c = nrm((BATCH, D), 1.0)
    ada_w = nrm((DEPTH, D, 6 * D), 0.2 * D ** -0.5)
    ada_b = nrm((DEPTH, 6 * D), 0.01)
    ln_w = 1.0 + nrm((DEPTH, 2, D), 0.02)
    ln_b = nrm((DEPTH, 2, D), 0.01)
    in_proj = nrm((N_EVEN, D, N_IN), D ** -0.5)
    conv_w = nrm((N_EVEN, CONV_K, CONV_CH), CONV_K ** -0.5)
    conv_b = nrm((N_EVEN, CONV_CH), 0.01)
    dt0 = jnp.exp(jax.random.uniform(next(ks), (N_EVEN, SSM_HEADS), f32, minval=math.log(1e-3), maxval=math.log(1e-1)))
    dt_bias = dt0 + jnp.log(-jnp.expm1(-dt0))
    a_log = jnp.log(jax.random.uniform(next(ks), (N_EVEN, SSM_HEADS), f32, minval=1.0, maxval=16.0))
    d_skip = 1.0 + nrm((N_EVEN, SSM_HEADS), 0.1)
    ssd_norm_w = 1.0 + nrm((N_EVEN, D_SSM), 0.02)
    pool_w = nrm((N_EVEN, N_POOL, POOL_GW, POOL_GW), POOL_GW ** -0.5)
    pool_scale = 1.0 + nrm((N_EVEN, D_POOL), 0.02)
    out_proj = nrm((N_EVEN, D_MIX, D), DN_BETA * D_MIX ** -0.5)
    wq_a = nrm((N_ODD, D, Q_LORA), D ** -0.5)
    q_norm_w = 1.0 + nrm((N_ODD, Q_LORA), 0.02)
    wq_b = nrm((N_ODD, Q_LORA, MLA_HEADS * (QK_NOPE + QK_ROPE)), Q_LORA ** -0.5)
    wkv_a = nrm((N_ODD, D, KV_LORA + QK_ROPE), D ** -0.5)
    kv_norm_w = 1.0 + nrm((N_ODD, KV_LORA), 0.02)
    wkv_b = nrm((N_ODD, KV_LORA, MLA_HEADS * (QK_NOPE + V_DIM)), KV_LORA ** -0.5)
    wo = nrm((N_ODD, MLA_HEADS * V_DIM, D), DN_BETA * (MLA_HEADS * V_DIM) ** -0.5)
    router_w = nrm((DEPTH, D, N_EXPERTS), D ** -0.5)
    router_b = nrm((DEPTH, N_EXPERTS), 0.01)
    w_gu = nrm((DEPTH, N_EXPERTS, D, 2 * D_EXPERT), D ** -0.5)
    b_gu = nrm((DEPTH, N_EXPERTS, 2 * D_EXPERT), 0.01)
    w_down = nrm((DEPTH, N_EXPERTS, D_EXPERT, D), DN_BETA * D_EXPERT ** -0.5)
    b_down = nrm((DEPTH, N_EXPERTS, D), 0.01)
    return {'x': x, 'c': c, 'ada_w': ada_w, 'ada_b': ada_b, 'ln_w': ln_w, 'ln_b': ln_b,
            'in_proj': in_proj, 'conv_w': conv_w, 'conv_b': conv_b, 'dt_bias': dt_bias, 'a_log': a_log,
            'd_skip': d_skip, 'ssd_norm_w': ssd_norm_w, 'pool_w': pool_w, 'pool_scale': pool_scale,
            'out_proj': out_proj, 'wq_a': wq_a, 'q_norm_w': q_norm_w, 'wq_b': wq_b, 'wkv_a': wkv_a,
            'kv_norm_w': kv_norm_w, 'wkv_b': wkv_b, 'wo': wo, 'router_w': router_w, 'router_b': router_b,
            'w_gu': w_gu, 'b_gu': b_gu, 'w_down': w_down, 'b_down': b_down}


def reference(x, c, ada_w, ada_b, ln_w, ln_b, in_proj, conv_w, conv_b, dt_bias, a_log, d_skip,
              ssd_norm_w, pool_w, pool_scale, out_proj, wq_a, q_norm_w, wq_b, wkv_a, kv_norm_w,
              wkv_b, wo, router_w, router_b, w_gu, b_gu, w_down, b_down):
    pos = jnp.arange(x.shape[1])
    cond = jax.nn.silu(c)
    for i in range(DEPTH):
        mod = (cond @ ada_w[i] + ada_b[i])[:, None, :]
        sh_a, sc_a, g_a, sh_f, sc_f, g_f = jnp.split(mod, 6, axis=-1)
        j = i // 2
        hmix = x * (1.0 + sc_a) + sh_a
        if i % 2 == 0:
            y = ssd_pool_mixer(hmix, in_proj[j], conv_w[j], conv_b[j], dt_bias[j], a_log[j], d_skip[j],
                               ssd_norm_w[j], pool_w[j], pool_scale[j], out_proj[j])
        else:
            y = mla_mixer(hmix, wq_a[j], q_norm_w[j], wq_b[j], wkv_a[j], kv_norm_w[j], wkv_b[j], wo[j], pos)
        x = layer_norm(DN_ALPHA * x + (1.0 + g_a) * y, ln_w[i, 0], ln_b[i, 0])
        hff = x * (1.0 + sc_f) + sh_f
        y = moe_ffn(hff, router_w[i], router_b[i], w_gu[i], b_gu[i], w_down[i], b_down[i])
        x = layer_norm(DN_ALPHA * x + (1.0 + g_f) * y, ln_w[i, 1], ln_b[i, 1])
    return x
```

```python
import functools
import math

import jax
import jax.numpy as jnp
from jax import lax
from jax.experimental import pallas as pl
from jax.experimental.pallas import tpu as pltpu

f32 = jnp.float32
bf16 = jnp.bfloat16

DEPTH = 2
DN_ALPHA = (2 * DEPTH) ** 0.25
LN_EPS = 1e-5
RMS_EPS = 1e-6
SSM_HEAD_DIM = 64
SSM_GROUPS = 8
SSM_STATE = 128
SSM_CHUNK = 128
CONV_K = 4
POOL_WINDOWS = (2, 4, 8, 16)
MLA_HEADS = 32
QK_NOPE = 128
QK_ROPE = 64
V_DIM = 128
ROPE_THETA = 10000.0
N_EXPERTS = 32
TOP_K = 4
SWIGLU_LIMIT = 7.0
SWIGLU_ALPHA = 1.702

LANES = 128
SUBLANES = 8
VMEM_LIMIT_BYTES = 56 * 1024 * 1024

MOE_TILE = 256


def _cparams(n_axes):
    return pltpu.CompilerParams(dimension_semantics=("arbitrary",) * n_axes, vmem_limit_bytes=VMEM_LIMIT_BYTES)


def _silu(v):
    return v * jax.nn.sigmoid(v)


def _ada_kernel(c_ref, w_ref, b_ref, o_ref):
    cond = _silu(c_ref[...])
    o_ref[...] = jnp.dot(cond.astype(bf16), w_ref[...].astype(bf16), preferred_element_type=f32) + b_ref[...]


def ada_modulation(c, ada_w, ada_b):
    B, D = c.shape
    L, _, N = ada_w.shape
    tn = 512
    cp = jnp.zeros((SUBLANES, D), f32).at[:B].set(c)
    out = pl.pallas_call(
        _ada_kernel,
        grid=(L, N // tn),
        in_specs=[pl.BlockSpec((SUBLANES, D), lambda l, j: (0, 0)),
                  pl.BlockSpec((None, D, tn), lambda l, j: (l, 0, j)),
                  pl.BlockSpec((None, 1, tn), lambda l, j: (l, 0, j))],
        out_specs=pl.BlockSpec((None, SUBLANES, tn), lambda l, j: (l, 0, j)),
        out_shape=jax.ShapeDtypeStruct((L, SUBLANES, N), f32),
        compiler_params=_cparams(2), name="ada_modulation",
    )(cp, ada_w, ada_b.reshape(L, 1, N))
    return out[:, :B].reshape(L, B, 6, D)


def _modulate_kernel(x_ref, sc_ref, sh_ref, o_ref):
    o_ref[...] = (x_ref[...] * (1.0 + sc_ref[...]) + sh_ref[...]).astype(o_ref.dtype)


def modulate(x, sc, sh, ts=512):
    B, S, D = x.shape
    row = pl.BlockSpec((None, ts, D), lambda b, i: (b, i, 0))
    per_b = pl.BlockSpec((None, 1, D), lambda b, i: (b, 0, 0))
    return pl.pallas_call(
        _modulate_kernel, grid=(B, S // ts), in_specs=[row, per_b, per_b], out_specs=row,
        out_shape=jax.ShapeDtypeStruct((B, S, D), bf16), compiler_params=_cparams(2), name="modulate",
    )(x, sc, sh)


def _mm_kernel(*refs, n_pairs):
    o_ref = refs[-1]
    acc = jnp.dot(refs[0][...], refs[n_pairs][...], preferred_element_type=f32)
    for p in range(1, n_pairs):
        acc += jnp.dot(refs[p][...], refs[n_pairs + p][...], preferred_element_type=f32)
    o_ref[...] = acc.astype(o_ref.dtype)


def matmul(xs, ws, out_dtype, tm, tn, name):
    M = xs[0].shape[0]
    N = ws[0].shape[1]
    tm = min(tm, M)
    assert M % tm == 0 and N % tn == 0
    in_specs = [pl.BlockSpec((tm, x.shape[1]), lambda i, j: (i, 0)) for x in xs]
    in_specs += [pl.BlockSpec((w.shape[0], tn), lambda i, j: (0, j)) for w in ws]
    return pl.pallas_call(
        functools.partial(_mm_kernel, n_pairs=len(xs)),
        grid=(M // tm, N // tn), in_specs=in_specs,
        out_specs=pl.BlockSpec((tm, tn), lambda i, j: (i, j)),
        out_shape=jax.ShapeDtypeStruct((M, N), out_dtype), compiler_params=_cparams(2), name=name,
    )(*xs, *ws)


def _conv_kernel(cur_ref, halo_ref, w_ref, b_ref, o_ref, buf_ref):
    ts = cur_ref.shape[0]
    cur = cur_ref[...]
    buf_ref[0:SUBLANES, :] = jnp.where(pl.program_id(1) == 0, 0.0, halo_ref[...])
    buf_ref[SUBLANES:SUBLANES + ts, :] = cur
    acc = b_ref[...] + w_ref[CONV_K - 1:CONV_K, :] * cur
    for k in range(CONV_K - 1):
        acc += w_ref[k:k + 1, :] * buf_ref[pl.ds(SUBLANES - (CONV_K - 1 - k), ts), :]
    o_ref[...] = _silu(acc).astype(o_ref.dtype)


def conv_silu(src, col0, n_ch, conv_w, conv_b, ch0, out_dtype, ts=512, tc=512):
    B, S, _ = src.shape
    cb0, wb0 = col0 // tc, ch0 // tc
    hb = ts // SUBLANES
    return pl.pallas_call(
        _conv_kernel, grid=(B, S // ts, n_ch // tc),
        in_specs=[pl.BlockSpec((None, ts, tc), lambda b, i, j: (b, i, cb0 + j)),
                  pl.BlockSpec((None, SUBLANES, tc), lambda b, i, j: (b, jnp.maximum(i * hb - 1, 0), cb0 + j)),
                  pl.BlockSpec((CONV_K, tc), lambda b, i, j: (0, wb0 + j)),
                  pl.BlockSpec((1, tc), lambda b, i, j: (0, wb0 + j))],
        out_specs=pl.BlockSpec((None, ts, tc), lambda b, i, j: (b, i, j)),
        out_shape=jax.ShapeDtypeStruct((B, S, n_ch), out_dtype),
        scratch_shapes=[pltpu.VMEM((ts + SUBLANES, tc), f32)],
        compiler_params=_cparams(3), name="conv_silu",
    )(src, src, conv_w, conv_b.reshape(1, -1))


def _ssd_kernel(xs_ref, z_ref, dt_ref, bc_ref, a_ref, dtb_ref, dskip_ref, nw_ref, o_ref, state_ref, y_ref):
    L = SSM_CHUNK
    P = SSM_HEAD_DIM
    G = SSM_GROUPS
    GW = xs_ref.shape[1] // G
    NS = SSM_STATE

    @pl.when(pl.program_id(1) == 0)
    def _():
        state_ref[...] = jnp.zeros_like(state_ref)

    raw = dt_ref[...] + dtb_ref[...]
    dt = jnp.maximum(raw, 0.0) + jnp.log1p(jnp.exp(-jnp.abs(raw)))
    la = dt * a_ref[...]
    row = lax.broadcasted_iota(jnp.int32, (L, LANES), 0)
    cum = la
    k = 1
    while k < L:
        cum = cum + jnp.where(row >= k, pltpu.roll(cum, k, axis=0), 0.0)
        k *= 2
    cum_t = cum.T
    last = cum[L - 1:L, :]
    to_end = jnp.exp(last - cum)
    e_cum = jnp.exp(cum)
    c_dec = jnp.exp(last)
    causal = row >= lax.broadcasted_iota(jnp.int32, (L, L), 1)
    left = lax.broadcasted_iota(jnp.int32, (L, LANES), 1) < P
    left1 = left[0:1, :]

    for g in range(G):
        bm = bc_ref[:, g * NS:(g + 1) * NS]
        cm = bc_ref[:, G * NS + g * NS:G * NS + (g + 1) * NS]
        cb = lax.dot_general(cm, bm, (((1,), (1,)), ((), ())), preferred_element_type=f32)
        h_in = state_ref[g]
        y_off = jnp.dot(cm, h_in.astype(bf16), preferred_element_type=f32)
        xw_parts = []
        cd_parts = []
        for j in range(GW // LANES):
            h0 = (g * GW) // P + 2 * j
            c0 = g * GW + j * LANES

            def pair(m, h0=h0):
                return jnp.where(left, m[:, h0:h0 + 1], m[:, h0 + 1:h0 + 2])

            xs = xs_ref[:, c0:c0 + LANES]
            xdt = xs * pair(dt)
            xdt_b = xdt.astype(bf16)
            dec = []
            for h in (h0, h0 + 1):
                seg = cum[:, h:h + 1] - cum_t[h:h + 1, :]
                dec.append((cb * jnp.exp(jnp.where(causal, seg, -jnp.inf))).astype(bf16))
            m_cat = jnp.concatenate(dec, axis=1)
            zero = jnp.zeros_like(xdt_b)
            x_blk = jnp.concatenate([jnp.where(left, xdt_b, zero), jnp.where(left, zero, xdt_b)], axis=0)
            y_diag = jnp.dot(m_cat, x_blk, preferred_element_type=f32)
            y = y_diag + y_off[:, j * LANES:(j + 1) * LANES] * pair(e_cum) + dskip_ref[:, c0:c0 + LANES] * xs
            y_ref[:, c0:c0 + LANES] = y
            xw_parts.append((xdt * pair(to_end)).astype(bf16))
            cd_parts.append(jnp.where(left1, c_dec[:, h0:h0 + 1], c_dec[:, h0 + 1:h0 + 2]))
        xw = jnp.concatenate(xw_parts, axis=1)
        bm_t = bm.astype(f32).T.astype(bf16)
        st = jnp.dot(bm_t, xw, preferred_element_type=f32)
        state_ref[g] = h_in * jnp.concatenate(cd_parts, axis=1) + st

    for g in range(G):
        sl = slice(g * GW, (g + 1) * GW)
        gz = y_ref[:, sl] * _silu(z_ref[:, sl])
        ms = jnp.mean(gz * gz, axis=-1, keepdims=True)
        o_ref[:, sl] = (gz * lax.rsqrt(ms + LN_EPS) * nw_ref[:, sl]).astype(o_ref.dtype)


def ssd_mixer(xs, zx, dt_raw, bc, a_row, dtb_row, dskip_row, norm_w):
    B, S, DS = xs.shape
    L = SSM_CHUNK
    GW = DS // SSM_GROUPS
    blk = lambda w: pl.BlockSpec((None, L, w), lambda b, c: (b, c, 0))
    vec = lambda w: pl.BlockSpec((1, w), lambda b, c: (0, 0))
    return pl.pallas_call(
        _ssd_kernel, grid=(B, S // L),
        in_specs=[blk(DS), blk(DS), blk(LANES), blk(bc.shape[2]), vec(LANES), vec(LANES), vec(DS), vec(DS)],
        out_specs=blk(DS),
        out_shape=jax.ShapeDtypeStruct((B, S, DS), bf16),
        scratch_shapes=[pltpu.VMEM((SSM_GROUPS, SSM_STATE, GW), f32), pltpu.VMEM((L, DS), f32)],
        compiler_params=_cparams(2), name="ssd_mixer",
    )(xs, zx, dt_raw, bc, a_row, dtb_row, dskip_row, norm_w.reshape(1, DS))


def _pool_kernel(u_ref, halo_ref, w_ref, sc_ref, o_ref, *, halo_rows):
    g = pl.program_id(0)
    i = pl.program_id(2)
    ts = u_ref.shape[0]
    win = jnp.int32(POOL_WINDOWS[-1])
    for gi in range(len(POOL_WINDOWS) - 2, -1, -1):
        win = jnp.where(g == gi, jnp.int32(POOL_WINDOWS[gi]), win)
    u = u_ref[...]
    halo = jnp.where(i == 0, 0.0, halo_ref[...])
    s = jnp.concatenate([halo, u], axis=0)
    k = 1
    while k < POOL_WINDOWS[-1]:
        s = jnp.where(win > k, s + pltpu.roll(s, k, axis=0), s)
        k *= 2
    s = s[halo_rows:, :]
    t1 = i * ts + 1 + lax.broadcasted_iota(jnp.int32, (ts, 1), 0)
    cnt = jnp.minimum(t1, win).astype(f32)
    diff = (s / cnt - u).astype(bf16)
    o_ref[...] = (jnp.dot(diff, w_ref[...], preferred_element_type=f32) * sc_ref[...]).astype(o_ref.dtype)


def pool_mixer(u, pool_w_b, pool_scale, ts=512):
    B, S, DP = u.shape
    NG, GW, _ = pool_w_b.shape
    halo_rows = POOL_WINDOWS[-1]
    hb = ts // halo_rows
    return pl.pallas_call(
        functools.partial(_pool_kernel, halo_rows=halo_rows), grid=(NG, B, S // ts),
        in_specs=[pl.BlockSpec((None, ts, GW), lambda g, b, i: (b, i, g)),
                  pl.BlockSpec((None, halo_rows, GW), lambda g, b, i: (b, jnp.maximum(i * hb - 1, 0), g)),
                  pl.BlockSpec((None, GW, GW), lambda g, b, i: (g, 0, 0)),
                  pl.BlockSpec((1, GW), lambda g, b, i: (0, g))],
        out_specs=pl.BlockSpec((None, ts, GW), lambda g, b, i: (b, i, g)),
        out_shape=jax.ShapeDtypeStruct((B, S, DP), bf16),
        compiler_params=_cparams(3), name="pool_mixer",
    )(u, u, pool_w_b, pool_scale.reshape(1, DP))


def _ln_kernel(*refs, with_h, with_router):
    x_ref, y_ref, g_ref, w_ref, b_ref = refs[:5]
    pos = 5
    v = DN_ALPHA * x_ref[...] + (1.0 + g_ref[...]) * y_ref[...].astype(f32)
    mu = jnp.mean(v, axis=-1, keepdims=True)
    d = v - mu
    var = jnp.mean(d * d, axis=-1, keepdims=True)
    xn = d * lax.rsqrt(var + LN_EPS) * w_ref[...] + b_ref[...]
    if with_h:
        sc_ref, sh_ref = refs[pos:pos + 2]
        pos += 2
    if with_router:
        rw_ref, rb_ref = refs[pos:pos + 2]
        pos += 2
    xo_ref = refs[pos]
    xo_ref[...] = xn
    if with_h:
        h = xn * (1.0 + sc_ref[...]) + sh_ref[...]
        refs[pos + 1][...] = h.astype(bf16)
        if with_router:
            refs[pos + 2][...] = jnp.dot(h, rw_ref[...], preferred_element_type=f32,
                                         precision=lax.Precision.HIGHEST) + rb_ref[...]


def post_ln(x, y, gate, ln_w, ln_b, sc=None, sh=None, router_w=None, router_b=None, ts=256):
    B, S, D = x.shape
    with_h = sc is not None
    with_router = router_w is not None
    row = pl.BlockSpec((None, ts, D), lambda b, i: (b, i, 0))
    per_b = pl.BlockSpec((None, 1, D), lambda b, i: (b, 0, 0))
    vec = pl.BlockSpec((1, D), lambda b, i: (0, 0))
    args = [x, y, gate, ln_w.reshape(1, D), ln_b.reshape(1, D)]
    in_specs = [row, row, per_b, vec, vec]
    out_shape = [jax.ShapeDtypeStruct((B, S, D), f32)]
    out_specs = [row]
    if with_h:
        args += [sc, sh]
        in_specs += [per_b, per_b]
        out_shape.append(jax.ShapeDtypeStruct((B, S, D), bf16))
        out_specs.append(row)
    if with_router:
        NE = router_w.shape[1]
        rw = jnp.zeros((D, LANES), f32).at[:, :NE].set(router_w)
        rb = jnp.zeros((1, LANES), f32).at[0, :NE].set(router_b)
        args += [rw, rb]
        in_specs += [pl.BlockSpec((D, LANES), lambda b, i: (0, 0)), pl.BlockSpec((1, LANES), lambda b, i: (0, 0))]
        out_shape.append(jax.ShapeDtypeStruct((B, S, LANES), f32))
        out_specs.append(pl.BlockSpec((None, ts, LANES), lambda b, i: (b, i, 0)))
    return pl.pallas_call(
        functools.partial(_ln_kernel, with_h=with_h, with_router=with_router),
        grid=(B, S // ts), in_specs=in_specs, out_specs=out_specs, out_shape=out_shape,
        compiler_params=_cparams(2), name="post_ln",
    )(*args)


def _expert_kernel(te_ref, nu_ref, x_ref, p_ref, wg_ref, wu_ref, bg_ref, bu_ref, wd_ref, bd_ref, o_ref):
    i = pl.program_id(0)

    @pl.when(i < nu_ref[0])
    def _():
        x = x_ref[...]
        gate = jnp.dot(x, wg_ref[...], preferred_element_type=f32) + bg_ref[...]
        up = jnp.dot(x, wu_ref[...], preferred_element_type=f32) + bu_ref[...]
        gate = jnp.minimum(gate, SWIGLU_LIMIT)
        up = jnp.clip(up, -SWIGLU_LIMIT, SWIGLU_LIMIT)
        glu = gate * jax.nn.sigmoid(SWIGLU_ALPHA * gate)
        h = ((up + 1.0) * glu).astype(bf16)
        out = jnp.dot(h, wd_ref[...], preferred_element_type=f32) + bd_ref[...]
        o_ref[...] = (out * p_ref[...]).astype(o_ref.dtype)

    @pl.when(i >= nu_ref[0])
    def _():
        o_ref[...] = jnp.zeros_like(o_ref)


def expert_ffn(tile_expert, n_used, xe, row_p, wg, wu, bg, bu, wd, bd):
    R, D = xe.shape
    E, _, F = wg.shape
    TM = MOE_TILE
    grid_spec = pltpu.PrefetchScalarGridSpec(
        num_scalar_prefetch=2, grid=(R // TM,),
        in_specs=[pl.BlockSpec((TM, D), lambda i, te, nu: (i, 0)),
                  pl.BlockSpec((TM, 1), lambda i, te, nu: (i, 0)),
                  pl.BlockSpec((None, D, F), lambda i, te, nu: (te[i], 0, 0)),
                  pl.BlockSpec((None, D, F), lambda i, te, nu: (te[i], 0, 0)),
                  pl.BlockSpec((None, 1, F), lambda i, te, nu: (te[i], 0, 0)),
                  pl.BlockSpec((None, 1, F), lambda i, te, nu: (te[i], 0, 0)),
                  pl.BlockSpec((None, F, D), lambda i, te, nu: (te[i], 0, 0)),
                  pl.BlockSpec((None, 1, D), lambda i, te, nu: (te[i], 0, 0))],
        out_specs=pl.BlockSpec((TM, D), lambda i, te, nu: (i, 0)))
    return pl.pallas_call(
        _expert_kernel, grid_spec=grid_spec, out_shape=jax.ShapeDtypeStruct((R, D), f32),
        compiler_params=_cparams(1), name="expert_ffn",
    )(tile_expert, n_used, xe, row_p, wg, wu, bg, bu, wd, bd)


def moe_ffn(h, logits, w_gu, b_gu, w_down, b_down):
    T, D = h.shape
    E, TM = N_EXPERTS, MOE_TILE
    top_v, top_e = lax.top_k(logits[:, :E], TOP_K)
    top_p = jax.nn.softmax(top_v, axis=-1)
    A = T * TOP_K
    flat_e = top_e.reshape(A)
    onehot = (flat_e[:, None] == jnp.arange(E, dtype=flat_e.dtype)[None, :]).astype(jnp.int32)
    csum = jnp.cumsum(onehot, axis=0)
    rank = jnp.take_along_axis(csum, flat_e[:, None], axis=1)[:, 0] - 1
    counts = csum[-1]
    ntile_e = (counts + TM - 1) // TM
    tile_end = jnp.cumsum(ntile_e)
    dest = (tile_end - ntile_e)[flat_e] * TM + rank
    n_tiles = (A + E * (TM - 1)) // TM
    R = n_tiles * TM
    row_tok = jnp.zeros((R,), jnp.int32).at[dest].set(jnp.arange(A, dtype=jnp.int32) // TOP_K)
    row_p = jnp.zeros((R,), f32).at[dest].set(top_p.reshape(A))
    tile_expert = jnp.minimum(jnp.searchsorted(tile_end, jnp.arange(n_tiles), side="right"), E - 1).astype(jnp.int32)
    n_used = tile_end[-1:].astype(jnp.int32)
    xe = h[row_tok]
    wg = w_gu[:, :, 0::2].astype(bf16)
    wu = w_gu[:, :, 1::2].astype(bf16)
    bg = b_gu[:, None, 0::2]
    bu = b_gu[:, None, 1::2]
    out = expert_ffn(tile_expert, n_used, xe, row_p.reshape(R, 1), wg, wu, bg, bu,
                     w_down.astype(bf16), b_down[:, None, :])
    return out[dest].reshape(T, TOP_K, D).sum(axis=1)


def _rope(v, cos, sin):
    n = v.shape[-1]
    even = lax.broadcasted_iota(jnp.int32, v.shape, 1) % 2 == 0
    partner = jnp.where(even, pltpu.roll(v, n - 1, axis=1), pltpu.roll(v, 1, axis=1))
    return v * cos + partner * sin


def _rms(v, w, eps):
    return v * lax.rsqrt(jnp.mean(v * v, axis=-1, keepdims=True) + eps) * w


def _mla_a_kernel(x_ref, wq_ref, wkv_ref, qw_ref, kw_ref, cos_ref, sin_ref, qn_ref, kvc_ref, kpe_ref, *, kv_lora):
    x = x_ref[...]
    qa = jnp.dot(x, wq_ref[...], preferred_element_type=f32)
    qn_ref[...] = _rms(qa, qw_ref[...], RMS_EPS).astype(bf16)
    kv = jnp.dot(x, wkv_ref[...], preferred_element_type=f32)
    kvc_ref[...] = _rms(kv[:, :kv_lora], kw_ref[...], RMS_EPS).astype(bf16)
    kpe_ref[...] = _rope(kv[:, kv_lora:], cos_ref[...], sin_ref[...]).astype(bf16)


def mla_down(h, wq_a_b, wkv_a_b, q_norm_w, kv_norm_w, cos_k, sin_k, S, tm=512):
    T, D = h.shape
    QL = wq_a_b.shape[1]
    KVW = wkv_a_b.shape[1]
    KL = kv_norm_w.shape[0]
    nsb = S // tm
    full = lambda a: pl.BlockSpec(a.shape, lambda i: (0, 0))
    return pl.pallas_call(
        functools.partial(_mla_a_kernel, kv_lora=KL), grid=(T // tm,),
        in_specs=[pl.BlockSpec((tm, D), lambda i: (i, 0)), full(wq_a_b), full(wkv_a_b),
                  pl.BlockSpec((1, QL), lambda i: (0, 0)), pl.BlockSpec((1, KL), lambda i: (0, 0)),
                  pl.BlockSpec((tm, KVW - KL), lambda i: (i % nsb, 0)),
                  pl.BlockSpec((tm, KVW - KL), lambda i: (i % nsb, 0))],
        out_specs=[pl.BlockSpec((tm, QL), lambda i: (i, 0)), pl.BlockSpec((tm, KL), lambda i: (i, 0)),
                   pl.BlockSpec((tm, KVW - KL), lambda i: (i, 0))],
        out_shape=[jax.ShapeDtypeStruct((T, QL), bf16), jax.ShapeDtypeStruct((T, KL), bf16),
                   jax.ShapeDtypeStruct((T, KVW - KL), bf16)],
        compiler_params=_cparams(1), name="mla_down",
    )(h, wq_a_b, wkv_a_b, q_norm_w.reshape(1, QL), kv_norm_w.reshape(1, KL), cos_k, sin_k)


def _q_up_kernel(x_ref, w_ref, cos_ref, sin_ref, o_ref):
    q = jnp.dot(x_ref[...], w_ref[...], preferred_element_type=f32)
    o_ref[...] = _rope(q, cos_ref[...], sin_ref[...]).astype(o_ref.dtype)


def q_up(qn, wq_heads, cos_q, sin_q, B, S, tm=512):
    T, QL = qn.shape
    H, _, W = wq_heads.shape
    nsb = S // tm
    return pl.pallas_call(
        _q_up_kernel, grid=(T // tm, H),
        in_specs=[pl.BlockSpec((tm, QL), lambda i, h: (i, 0)),
                  pl.BlockSpec((None, QL, W), lambda i, h: (h, 0, 0)),
                  pl.BlockSpec((tm, W), lambda i, h: (i % nsb, 0)),
                  pl.BlockSpec((tm, W), lambda i, h: (i % nsb, 0))],
        out_specs=pl.BlockSpec((None, None, tm, W), lambda i, h: (i // nsb, h, i % nsb, 0)),
        out_shape=jax.ShapeDtypeStruct((B, H, S, W), bf16),
        compiler_params=_cparams(2), name="q_up",
    )(qn, wq_heads, cos_q, sin_q)


def _kv_up_kernel(c_ref, w_ref, kpe_ref, k_ref, v_ref):
    kv = jnp.dot(c_ref[...], w_ref[...], preferred_element_type=f32)
    k_ref[...] = jnp.concatenate([kv[:, :QK_NOPE].astype(bf16), kpe_ref[...]], axis=1)
    v_ref[...] = kv[:, QK_NOPE:].astype(bf16)


def kv_up(kvc, wkv_heads, kpe, B, S, tm=512):
    T, KL = kvc.shape
    H, _, W = wkv_heads.shape
    nsb = S // tm
    return pl.pallas_call(
        _kv_up_kernel, grid=(T // tm, H),
        in_specs=[pl.BlockSpec((tm, KL), lambda i, h: (i, 0)),
                  pl.BlockSpec((None, KL, W), lambda i, h: (h, 0, 0)),
                  pl.BlockSpec((tm, LANES), lambda i, h: (i, 0))],
        out_specs=[pl.BlockSpec((None, None, tm, QK_NOPE + LANES), lambda i, h: (i // nsb, h, i % nsb, 0)),
                   pl.BlockSpec((None, None, tm, V_DIM), lambda i, h: (i // nsb, h, i % nsb, 0))],
        out_shape=[jax.ShapeDtypeStruct((B, H, S, QK_NOPE + LANES), bf16),
                   jax.ShapeDtypeStruct((B, H, S, V_DIM), bf16)],
        compiler_params=_cparams(2), name="kv_up",
    )(kvc, wkv_heads, kpe)


def _attn_kernel(q_ref, k_ref, v_ref, o_ref, *, tq, scale):
    i = pl.program_id(2)
    q = q_ref[...]

    def step(j, carry, masked):
        m, l, acc = carry
        start = pl.multiple_of(j * tq, tq)
        kj = k_ref[pl.ds(start, tq), :]
        vj = v_ref[pl.ds(start, tq), :]
        s = lax.dot_general(q, kj, (((1,), (1,)), ((), ())), preferred_element_type=f32) * scale
        if masked:
            rows = lax.broadcasted_iota(jnp.int32, (tq, tq), 0)
            cols = lax.broadcasted_iota(jnp.int32, (tq, tq), 1)
            s = jnp.where(rows >= cols, s, -jnp.inf)
        m_new = jnp.maximum(m, jnp.max(s, axis=-1, keepdims=True))
        alpha = jnp.exp(m - m_new)
        p = jnp.exp(s - m_new)
        l = alpha * l + jnp.sum(p, axis=-1, keepdims=True)
        acc = alpha * acc + jnp.dot(p.astype(bf16), vj, preferred_element_type=f32)
        return m_new, l, acc

    init = (jnp.full((tq, 1), -jnp.inf, f32), jnp.zeros((tq, 1), f32), jnp.zeros((tq, v_ref.shape[1]), f32))
    carry = lax.fori_loop(0, i, lambda j, c: step(j, c, False), init)
    _, l, acc = step(i, carry, True)
    o_ref[...] = (acc / l).astype(o_ref.dtype)


def causal_attention(q, k, v, scale, tq=256):
    B, H, S, W = q.shape
    VD = v.shape[3]
    return pl.pallas_call(
        functools.partial(_attn_kernel, tq=tq, scale=scale), grid=(B, H, S // tq),
        in_specs=[pl.BlockSpec((None, None, tq, W), lambda b, h, i: (b, h, i, 0)),
                  pl.BlockSpec((None, None, S, W), lambda b, h, i: (b, h, 0, 0)),
                  pl.BlockSpec((None, None, S, VD), lambda b, h, i: (b, h, 0, 0))],
        out_specs=pl.BlockSpec((None, tq, VD), lambda b, h, i: (b, i, h)),
        out_shape=jax.ShapeDtypeStruct((B, S, H * VD), bf16),
        compiler_params=_cparams(3), name="causal_attention",
    )(q, k, v)


def _rope_tables(S, width, rope0):
    half = QK_ROPE // 2
    inv_freq = ROPE_THETA ** (-jnp.arange(half, dtype=f32) * 2.0 / QK_ROPE)
    ang = jnp.arange(S).astype(f32)[:, None] * inv_freq[None, :]
    cos = jnp.repeat(jnp.cos(ang), 2, axis=1)
    sin = jnp.repeat(jnp.sin(ang), 2, axis=1) * jnp.tile(jnp.array([-1.0, 1.0], f32), half)[None, :]
    cos_t = jnp.ones((S, width), f32).at[:, rope0:rope0 + QK_ROPE].set(cos)
    sin_t = jnp.zeros((S, width), f32).at[:, rope0:rope0 + QK_ROPE].set(sin)
    return cos_t, sin_t


def mla_mixer(h, wq_a, q_norm_w, wq_b, wkv_a, kv_norm_w, wkv_b, wo, B, S):
    T, D = h.shape
    H = MLA_HEADS
    KL = kv_norm_w.shape[0]
    QL = wq_a.shape[1]
    W = 2 * LANES
    wkv_a_b = jnp.zeros((D, KL + LANES), f32).at[:, :KL + QK_ROPE].set(wkv_a).astype(bf16)
    wq_heads = jnp.zeros((H, QL, W), f32).at[:, :, :QK_NOPE + QK_ROPE].set(
        wq_b.reshape(QL, H, QK_NOPE + QK_ROPE).transpose(1, 0, 2)).astype(bf16)
    wkv_heads = wkv_b.reshape(KL, H, QK_NOPE + V_DIM).transpose(1, 0, 2).astype(bf16)
    cos_k, sin_k = _rope_tables(S, LANES, 0)
    cos_q, sin_q = _rope_tables(S, W, QK_NOPE)
    qn, kvc, kpe = mla_down(h, wq_a.astype(bf16), wkv_a_b, q_norm_w, kv_norm_w, cos_k, sin_k, S)
    q = q_up(qn, wq_heads, cos_q, sin_q, B, S)
    k, v = kv_up(kvc, wkv_heads, kpe, B, S)
    o = causal_attention(q, k, v, (QK_NOPE + QK_ROPE) ** -0.5)
    return matmul([o.reshape(T, H * V_DIM)], [wo.astype(bf16)], f32, 1024, 512, "attn_out_proj")


def ssd_pool_mixer(h, in_proj, conv_w, conv_b, dt_bias, a_log, d_skip, norm_w, pool_w, pool_scale, out_proj, B, S):
    T, D = h.shape
    DS = norm_w.shape[0]
    NH = dt_bias.shape[0]
    GN = SSM_GROUPS * SSM_STATE
    o1 = DS + DS + 2 * GN
    o2 = o1 + NH
    w_zx = in_proj[:, :o1].astype(bf16)
    w_dt = jnp.zeros((D, LANES), f32).at[:, :NH].set(in_proj[:, o1:o2]).astype(bf16)
    w_u = in_proj[:, o2:].astype(bf16)
    zx = matmul([h], [w_zx], f32, 1024, 512, "in_proj_zx").reshape(B, S, o1)
    dt_raw = matmul([h], [w_dt], f32, 1024, LANES, "in_proj_dt").reshape(B, S, LANES)
    u = matmul([h], [w_u], f32, 1024, 512, "in_proj_u").reshape(B, S, -1)
    xs = conv_silu(zx, DS, DS, conv_w, conv_b, 0, f32)
    bc = conv_silu(zx, 2 * DS, 2 * GN, conv_w, conv_b, DS, bf16)
    pad = lambda v: jnp.zeros((1, LANES), f32).at[0, :NH].set(v)
    a_row = pad(-jnp.exp(a_log))
    dskip_row = jnp.repeat(d_skip, SSM_HEAD_DIM).reshape(1, DS)
    y = ssd_mixer(xs, zx, dt_raw, bc, a_row, pad(dt_bias), dskip_row, norm_w)
    y_pool = pool_mixer(u, pool_w.astype(bf16), pool_scale)
    w_out = out_proj.astype(bf16)
    return matmul([y.reshape(T, DS), y_pool.reshape(T, -1)], [w_out[:DS], w_out[DS:]], f32, 1024, 256, "mixer_out_proj")


def kernel(x, c, ada_w, ada_b, ln_w, ln_b, in_proj, conv_w, conv_b, dt_bias, a_log, d_skip, ssd_norm_w, pool_w, pool_scale, out_proj, wq_a, q_norm_w, wq_b, wkv_a, kv_norm_w, wkv_b, wo, router_w, router_b, w_gu, b_gu, w_down, b_down):
    B, S, D = x.shape
    T = B * S
    mod = ada_modulation(c, ada_w, ada_b)
    m = lambda l, k: mod[l, :, k].reshape(B, 1, D)
    h = modulate(x, m(0, 1), m(0, 0))
    for i in range(DEPTH):
        j = i // 2
        hf = h.reshape(T, D)
        if i % 2 == 0:
            y = ssd_pool_mixer(hf, in_proj[j], conv_w[j], conv_b[j], dt_bias[j], a_log[j], d_skip[j],
                               ssd_norm_w[j], pool_w[j], pool_scale[j], out_proj[j], B, S)
        else:
            y = mla_mixer(hf, wq_a[j], q_norm_w[j], wq_b[j], wkv_a[j], kv_norm_w[j], wkv_b[j], wo[j], B, S)
        x, hff, logits = post_ln(x, y.reshape(B, S, D), m(i, 2), ln_w[i, 0], ln_b[i, 0], m(i, 4), m(i, 3),
                                 router_w[i], router_b[i])
        y = moe_ffn(hff.reshape(T, D), logits.reshape(T, LANES), w_gu[i], b_gu[i], w_down[i], b_down[i])
        if i + 1 < DEPTH:
            x, h = post_ln(x, y.reshape(B, S, D), m(i, 5), ln_w[i, 1], ln_b[i, 1], m(i + 1, 1), m(i + 1, 0))
        else:
            (x,) = post_ln(x, y.reshape(B, S, D), m(i, 5), ln_w[i, 1], ln_b[i, 1])
    return x
```

```python
import functools
import math

import jax
import jax.numpy as jnp
from jax import lax
from jax.experimental import pallas as pl
from jax.experimental.pallas import tpu as pltpu

f32 = jnp.float32
bf16 = jnp.bfloat16

DEPTH = 2
DN_ALPHA = (2 * DEPTH) ** 0.25
LN_EPS = 1e-5
RMS_EPS = 1e-6
SSM_HEAD_DIM = 64
SSM_GROUPS = 8
SSM_STATE = 128
SSM_CHUNK = 128
CONV_K = 4
POOL_WINDOWS = (2, 4, 8, 16)
MLA_HEADS = 32
QK_NOPE = 128
QK_ROPE = 64
V_DIM = 128
ROPE_THETA = 10000.0
N_EXPERTS = 32
TOP_K = 4
SWIGLU_LIMIT = 7.0
SWIGLU_ALPHA = 1.702

LANES = 128
SUBLANES = 8
VMEM_LIMIT_BYTES = 56 * 1024 * 1024

MOE_TILE = 256


def _cparams(n_axes):
    return pltpu.CompilerParams(dimension_semantics=("arbitrary",) * n_axes, vmem_limit_bytes=VMEM_LIMIT_BYTES)


def _silu(v):
    return v * jax.nn.sigmoid(v)


def _ada_kernel(c_ref, w_ref, b_ref, o_ref):
    cond = _silu(c_ref[...])
    o_ref[...] = jnp.dot(cond.astype(bf16), w_ref[...].astype(bf16), preferred_element_type=f32) + b_ref[...]


def ada_modulation(c, ada_w, ada_b):
    B, D = c.shape
    L, _, N = ada_w.shape
    tn = 512
    cp = jnp.zeros((SUBLANES, D), f32).at[:B].set(c)
    out = pl.pallas_call(
        _ada_kernel,
        grid=(L, N // tn),
        in_specs=[pl.BlockSpec((SUBLANES, D), lambda l, j: (0, 0)),
                  pl.BlockSpec((None, D, tn), lambda l, j: (l, 0, j)),
                  pl.BlockSpec((None, 1, tn), lambda l, j: (l, 0, j))],
        out_specs=pl.BlockSpec((None, SUBLANES, tn), lambda l, j: (l, 0, j)),
        out_shape=jax.ShapeDtypeStruct((L, SUBLANES, N), f32),
        compiler_params=_cparams(2), name="ada_modulation",
    )(cp, ada_w, ada_b.reshape(L, 1, N))
    return out[:, :B].reshape(L, B, 6, D)


def _modulate_kernel(x_ref, sc_ref, sh_ref, o_ref):
    o_ref[...] = (x_ref[...] * (1.0 + sc_ref[...]) + sh_ref[...]).astype(o_ref.dtype)


def modulate(x, sc, sh, ts=512):
    B, S, D = x.shape
    row = pl.BlockSpec((None, ts, D), lambda b, i: (b, i, 0))
    per_b = pl.BlockSpec((None, 1, D), lambda b, i: (b, 0, 0))
    return pl.pallas_call(
        _modulate_kernel, grid=(B, S // ts), in_specs=[row, per_b, per_b], out_specs=row,
        out_shape=jax.ShapeDtypeStruct((B, S, D), bf16), compiler_params=_cparams(2), name="modulate",
    )(x, sc, sh)


def _mm_kernel(*refs, n_pairs):
    o_ref = refs[-1]
    acc = jnp.dot(refs[0][...], refs[n_pairs][...], preferred_element_type=f32)
    for p in range(1, n_pairs):
        acc += jnp.dot(refs[p][...], refs[n_pairs + p][...], preferred_element_type=f32)
    o_ref[...] = acc.astype(o_ref.dtype)


def matmul(xs, ws, out_dtype, tm, tn, name):
    M = xs[0].shape[0]
    N = ws[0].shape[1]
    tm = min(tm, M)
    assert M % tm == 0 and N % tn == 0
    in_specs = [pl.BlockSpec((tm, x.shape[1]), lambda i, j: (i, 0)) for x in xs]
    in_specs += [pl.BlockSpec((w.shape[0], tn), lambda i, j: (0, j)) for w in ws]
    return pl.pallas_call(
        functools.partial(_mm_kernel, n_pairs=len(xs)),
        grid=(M // tm, N // tn), in_specs=in_specs,
        out_specs=pl.BlockSpec((tm, tn), lambda i, j: (i, j)),
        out_shape=jax.ShapeDtypeStruct((M, N), out_dtype), compiler_params=_cparams(2), name=name,
    )(*xs, *ws)


def _conv_kernel(cur_ref, halo_ref, w_ref, b_ref, o_ref, buf_ref):
    ts = cur_ref.shape[0]
    cur = cur_ref[...]
    buf_ref[0:SUBLANES, :] = jnp.where(pl.program_id(1) == 0, 0.0, halo_ref[...])
    buf_ref[SUBLANES:SUBLANES + ts, :] = cur
    acc = b_ref[...] + w_ref[CONV_K - 1:CONV_K, :] * cur
    for k in range(CONV_K - 1):
        acc += w_ref[k:k + 1, :] * buf_ref[pl.ds(SUBLANES - (CONV_K - 1 - k), ts), :]
    o_ref[...] = _silu(acc).astype(o_ref.dtype)


def conv_silu(src, col0, n_ch, conv_w, conv_b, ch0, out_dtype, ts=512, tc=512):
    B, S, _ = src.shape
    cb0, wb0 = col0 // tc, ch0 // tc
    hb = ts // SUBLANES
    return pl.pallas_call(
        _conv_kernel, grid=(B, S // ts, n_ch // tc),
        in_specs=[pl.BlockSpec((None, ts, tc), lambda b, i, j: (b, i, cb0 + j)),
                  pl.BlockSpec((None, SUBLANES, tc), lambda b, i, j: (b, jnp.maximum(i * hb - 1, 0), cb0 + j)),
                  pl.BlockSpec((CONV_K, tc), lambda b, i, j: (0, wb0 + j)),
                  pl.BlockSpec((1, tc), lambda b, i, j: (0, wb0 + j))],
        out_specs=pl.BlockSpec((None, ts, tc), lambda b, i, j: (b, i, j)),
        out_shape=jax.ShapeDtypeStruct((B, S, n_ch), out_dtype),
        scratch_shapes=[pltpu.VMEM((ts + SUBLANES, tc), f32)],
        compiler_params=_cparams(3), name="conv_silu",
    )(src, src, conv_w, conv_b.reshape(1, -1))


def _ssd_kernel(xs_ref, z_ref, dt_ref, bc_ref, a_ref, dtb_ref, dskip_ref, nw_ref, o_ref, state_ref, y_ref):
    L = SSM_CHUNK
    P = SSM_HEAD_DIM
    G = SSM_GROUPS
    GW = xs_ref.shape[1] // G
    NS = SSM_STATE

    @pl.when(pl.program_id(1) == 0)
    def _():
        state_ref[...] = jnp.zeros_like(state_ref)

    raw = dt_ref[...] + dtb_ref[...]
    dt = jnp.maximum(raw, 0.0) + jnp.log1p(jnp.exp(-jnp.abs(raw)))
    la = dt * a_ref[...]
    row = lax.broadcasted_iota(jnp.int32, (L, LANES), 0)
    cum = la
    k = 1
    while k < L:
        cum = cum + jnp.where(row >= k, pltpu.roll(cum, k, axis=0), 0.0)
        k *= 2
    cum_t = cum.T
    last = cum[L - 1:L, :]
    to_end = jnp.exp(last - cum)
    e_cum = jnp.exp(cum)
    c_dec = jnp.exp(last)
    causal = row >= lax.broadcasted_iota(jnp.int32, (L, L), 1)
    left = lax.broadcasted_iota(jnp.int32, (L, LANES), 1) < P
    left1 = left[0:1, :]

    for g in range(G):
        bm = bc_ref[:, g * NS:(g + 1) * NS]
        cm = bc_ref[:, G * NS + g * NS:G * NS + (g + 1) * NS]
        cb = lax.dot_general(cm, bm, (((1,), (1,)), ((), ())), preferred_element_type=f32)
        h_in = state_ref[g]
        y_off = jnp.dot(cm, h_in.astype(bf16), preferred_element_type=f32)
        xw_parts = []
        cd_parts = []
        for j in range(GW // LANES):
            h0 = (g * GW) // P + 2 * j
            c0 = g * GW + j * LANES

            def pair(m, h0=h0):
                return jnp.where(left, m[:, h0:h0 + 1], m[:, h0 + 1:h0 + 2])

            xs = xs_ref[:, c0:c0 + LANES]
            xdt = xs * pair(dt)
            xdt_b = xdt.astype(bf16)
            dec = []
            for h in (h0, h0 + 1):
                seg = cum[:, h:h + 1] - cum_t[h:h + 1, :]
                dec.append((cb * jnp.exp(jnp.where(causal, seg, -jnp.inf))).astype(bf16))
            m_cat = jnp.concatenate(dec, axis=1)
            zero = jnp.zeros_like(xdt_b)
            x_blk = jnp.concatenate([jnp.where(left, xdt_b, zero), jnp.where(left, zero, xdt_b)], axis=0)
            y_diag = jnp.dot(m_cat, x_blk, preferred_element_type=f32)
            y = y_diag + y_off[:, j * LANES:(j + 1) * LANES] * pair(e_cum) + dskip_ref[:, c0:c0 + LANES] * xs
            y_ref[:, c0:c0 + LANES] = y
            xw_parts.append((xdt * pair(to_end)).astype(bf16))
            cd_parts.append(jnp.where(left1, c_dec[:, h0:h0 + 1], c_dec[:, h0 + 1:h0 + 2]))
        xw = jnp.concatenate(xw_parts, axis=1)
        bm_t = bm.astype(f32).T.astype(bf16)
        st = jnp.dot(bm_t, xw, preferred_element_type=f32)
        state_ref[g] = h_in * jnp.concatenate(cd_parts, axis=1) + st

    for g in range(G):
        sl = slice(g * GW, (g + 1) * GW)
        gz = y_ref[:, sl] * _silu(z_ref[:, sl])
        ms = jnp.mean(gz * gz, axis=-1, keepdims=True)
        o_ref[:, sl] = (gz * lax.rsqrt(ms + LN_EPS) * nw_ref[:, sl]).astype(o_ref.dtype)


def ssd_mixer(xs, zx, dt_raw, bc, a_row, dtb_row, dskip_row, norm_w):
    B, S, DS = xs.shape
    L = SSM_CHUNK
    GW = DS // SSM_GROUPS
    blk = lambda w: pl.BlockSpec((None, L, w), lambda b, c: (b, c, 0))
    vec = lambda w: pl.BlockSpec((1, w), lambda b, c: (0, 0))
    return pl.pallas_call(
        _ssd_kernel, grid=(B, S // L),
        in_specs=[blk(DS), blk(DS), blk(LANES), blk(bc.shape[2]), vec(LANES), vec(LANES), vec(DS), vec(DS)],
        out_specs=blk(DS),
        out_shape=jax.ShapeDtypeStruct((B, S, DS), bf16),
        scratch_shapes=[pltpu.VMEM((SSM_GROUPS, SSM_STATE, GW), f32), pltpu.VMEM((L, DS), f32)],
        compiler_params=_cparams(2), name="ssd_mixer",
    )(xs, zx, dt_raw, bc, a_row, dtb_row, dskip_row, norm_w.reshape(1, DS))


def _pool_kernel(u_ref, halo_ref, w_ref, sc_ref, o_ref, *, halo_rows):
    g = pl.program_id(0)
    i = pl.program_id(2)
    ts = u_ref.shape[0]
    win = jnp.int32(POOL_WINDOWS[-1])
    for gi in range(len(POOL_WINDOWS) - 2, -1, -1):
        win = jnp.where(g == gi, jnp.int32(POOL_WINDOWS[gi]), win)
    u = u_ref[...]
    halo = jnp.where(i == 0, 0.0, halo_ref[...])
    s = jnp.concatenate([halo, u], axis=0)
    k = 1
    while k < POOL_WINDOWS[-1]:
        s = jnp.where(win > k, s + pltpu.roll(s, k, axis=0), s)
        k *= 2
    s = s[halo_rows:, :]
    t1 = i * ts + 1 + lax.broadcasted_iota(jnp.int32, (ts, 1), 0)
    cnt = jnp.minimum(t1, win).astype(f32)
    diff = (s / cnt - u).astype(bf16)
    o_ref[...] = (jnp.dot(diff, w_ref[...], preferred_element_type=f32) * sc_ref[...]).astype(o_ref.dtype)


def pool_mixer(u, pool_w_b, pool_scale, ts=512):
    B, S, DP = u.shape
    NG, GW, _ = pool_w_b.shape
    halo_rows = POOL_WINDOWS[-1]
    hb = ts // halo_rows
    return pl.pallas_call(
        functools.partial(_pool_kernel, halo_rows=halo_rows), grid=(NG, B, S // ts),
        in_specs=[pl.BlockSpec((None, ts, GW), lambda g, b, i: (b, i, g)),
                  pl.BlockSpec((None, halo_rows, GW), lambda g, b, i: (b, jnp.maximum(i * hb - 1, 0), g)),
                  pl.BlockSpec((None, GW, GW), lambda g, b, i: (g, 0, 0)),
                  pl.BlockSpec((1, GW), lambda g, b, i: (0, g))],
        out_specs=pl.BlockSpec((None, ts, GW), lambda g, b, i: (b, i, g)),
        out_shape=jax.ShapeDtypeStruct((B, S, DP), bf16),
        compiler_params=_cparams(3), name="pool_mixer",
    )(u, u, pool_w_b, pool_scale.reshape(1, DP))


def _ln_kernel(*refs, n_y, with_h, with_router):
    x_ref = refs[0]
    y_refs = refs[1:1 + n_y]
    g_ref, w_ref, b_ref = refs[1 + n_y:4 + n_y]
    pos = 4 + n_y
    y = y_refs[0][...].astype(f32)
    for r in y_refs[1:]:
        y = y + r[...].astype(f32)
    v = DN_ALPHA * x_ref[...] + (1.0 + g_ref[...]) * y
    mu = jnp.mean(v, axis=-1, keepdims=True)
    d = v - mu
    var = jnp.mean(d * d, axis=-1, keepdims=True)
    xn = d * lax.rsqrt(var + LN_EPS) * w_ref[...] + b_ref[...]
    if with_h:
        sc_ref, sh_ref = refs[pos:pos + 2]
        pos += 2
    if with_router:
        rw_ref, rb_ref = refs[pos:pos + 2]
        pos += 2
    xo_ref = refs[pos]
    xo_ref[...] = xn
    if with_h:
        h = xn * (1.0 + sc_ref[...]) + sh_ref[...]
        refs[pos + 1][...] = h.astype(bf16)
        if with_router:
            refs[pos + 2][...] = jnp.dot(h, rw_ref[...], preferred_element_type=f32,
                                         precision=lax.Precision.HIGHEST) + rb_ref[...]


def post_ln(x, ys, gate, ln_w, ln_b, sc=None, sh=None, router_w=None, router_b=None, ts=256):
    B, S, D = x.shape
    with_h = sc is not None
    with_router = router_w is not None
    row = pl.BlockSpec((None, ts, D), lambda b, i: (b, i, 0))
    per_b = pl.BlockSpec((None, 1, D), lambda b, i: (b, 0, 0))
    vec = pl.BlockSpec((1, D), lambda b, i: (0, 0))
    args = [x, *ys, gate, ln_w.reshape(1, D), ln_b.reshape(1, D)]
    in_specs = [row] * (1 + len(ys)) + [per_b, vec, vec]
    out_shape = [jax.ShapeDtypeStruct((B, S, D), f32)]
    out_specs = [row]
    if with_h:
        args += [sc, sh]
        in_specs += [per_b, per_b]
        out_shape.append(jax.ShapeDtypeStruct((B, S, D), bf16))
        out_specs.append(row)
    if with_router:
        NE = router_w.shape[1]
        rw = jnp.zeros((D, LANES), f32).at[:, :NE].set(router_w)
        rb = jnp.zeros((1, LANES), f32).at[0, :NE].set(router_b)
        args += [rw, rb]
        in_specs += [pl.BlockSpec((D, LANES), lambda b, i: (0, 0)), pl.BlockSpec((1, LANES), lambda b, i: (0, 0))]
        out_shape.append(jax.ShapeDtypeStruct((B, S, LANES), f32))
        out_specs.append(pl.BlockSpec((None, ts, LANES), lambda b, i: (b, i, 0)))
    return pl.pallas_call(
        functools.partial(_ln_kernel, n_y=len(ys), with_h=with_h, with_router=with_router),
        grid=(B, S // ts), in_specs=in_specs, out_specs=out_specs, out_shape=out_shape,
        compiler_params=_cparams(2), name="post_ln",
    )(*args)


def _expert_kernel(te_ref, nu_ref, x_ref, p_ref, wgu_ref, bgu_ref, sel_ref, wd_ref, bd_ref, o_ref):
    i = pl.program_id(0)

    @pl.when(i < nu_ref[0])
    def _():
        gu = jnp.dot(x_ref[...], wgu_ref[...], preferred_element_type=f32) + bgu_ref[...]
        gate = jnp.minimum(gu, SWIGLU_LIMIT)
        up = jnp.clip(gu, -SWIGLU_LIMIT, SWIGLU_LIMIT)
        glu = gate * jax.nn.sigmoid(SWIGLU_ALPHA * gate)
        up_next = pltpu.roll(up, up.shape[1] - 1, axis=1)
        hf = ((up_next + 1.0) * glu).astype(bf16)
        h = jnp.dot(hf, sel_ref[...], preferred_element_type=f32)
        out = jnp.dot(h.astype(bf16), wd_ref[...], preferred_element_type=f32) + bd_ref[...]
        o_ref[...] = (out * p_ref[...]).astype(o_ref.dtype)

    @pl.when(i >= nu_ref[0])
    def _():
        o_ref[...] = jnp.zeros_like(o_ref)


def expert_ffn(tile_expert, n_used, xe, row_p, wgu, bgu, sel, wd, bd):
    R, D = xe.shape
    E, _, F2 = wgu.shape
    F = F2 // 2
    TM = MOE_TILE
    grid_spec = pltpu.PrefetchScalarGridSpec(
        num_scalar_prefetch=2, grid=(R // TM,),
        in_specs=[pl.BlockSpec((TM, D), lambda i, te, nu: (i, 0)),
                  pl.BlockSpec((TM, 1), lambda i, te, nu: (i, 0)),
                  pl.BlockSpec((None, D, F2), lambda i, te, nu: (te[i], 0, 0)),
                  pl.BlockSpec((None, 1, F2), lambda i, te, nu: (te[i], 0, 0)),
                  pl.BlockSpec((F2, F), lambda i, te, nu: (0, 0)),
                  pl.BlockSpec((None, F, D), lambda i, te, nu: (te[i], 0, 0)),
                  pl.BlockSpec((None, 1, D), lambda i, te, nu: (te[i], 0, 0))],
        out_specs=pl.BlockSpec((TM, D), lambda i, te, nu: (i, 0)))
    return pl.pallas_call(
        _expert_kernel, grid_spec=grid_spec, out_shape=jax.ShapeDtypeStruct((R, D), bf16),
        compiler_params=_cparams(1), name="expert_ffn",
    )(tile_expert, n_used, xe, row_p, wgu, bgu, sel, wd, bd)


def moe_ffn(h, logits, w_gu, b_gu, w_down, b_down):
    T, D = h.shape
    E, TM = N_EXPERTS, MOE_TILE
    F2 = w_gu.shape[2]
    top_v, top_e = lax.top_k(logits[:, :E], TOP_K)
    top_p = jax.nn.softmax(top_v, axis=-1)
    A = T * TOP_K
    flat_e = top_e.reshape(A)
    onehot = (flat_e[:, None] == jnp.arange(E, dtype=flat_e.dtype)[None, :]).astype(jnp.int32)
    csum = jnp.cumsum(onehot, axis=0)
    rank = jnp.take_along_axis(csum, flat_e[:, None], axis=1)[:, 0] - 1
    counts = csum[-1]
    ntile_e = (counts + TM - 1) // TM
    tile_end = jnp.cumsum(ntile_e)
    dest = (tile_end - ntile_e)[flat_e] * TM + rank
    n_tiles = (A + E * (TM - 1)) // TM
    R = n_tiles * TM
    row_tok = jnp.zeros((R,), jnp.int32).at[dest].set(jnp.arange(A, dtype=jnp.int32) // TOP_K)
    row_p = jnp.zeros((R,), f32).at[dest].set(top_p.reshape(A))
    tile_expert = jnp.minimum(jnp.searchsorted(tile_end, jnp.arange(n_tiles), side="right"), E - 1).astype(jnp.int32)
    n_used = tile_end[-1:].astype(jnp.int32)
    xe = h[row_tok]
    sel = (jnp.arange(F2)[:, None] == 2 * jnp.arange(F2 // 2)[None, :]).astype(bf16)
    out = expert_ffn(tile_expert, n_used, xe, row_p.reshape(R, 1), w_gu.astype(bf16), b_gu[:, None, :], sel,
                     w_down.astype(bf16), b_down[:, None, :])
    dest_k = dest.reshape(T, TOP_K)
    return [out[dest_k[:, k]] for k in range(TOP_K)]


def _rope(v, cos, sin):
    n = v.shape[-1]
    even = lax.broadcasted_iota(jnp.int32, v.shape, 1) % 2 == 0
    partner = jnp.where(even, pltpu.roll(v, n - 1, axis=1), pltpu.roll(v, 1, axis=1))
    return v * cos + partner * sin


def _rms(v, w, eps):
    return v * lax.rsqrt(jnp.mean(v * v, axis=-1, keepdims=True) + eps) * w


def _mla_a_kernel(x_ref, wq_ref, wkv_ref, qw_ref, kw_ref, cos_ref, sin_ref, qn_ref, kvc_ref, kpe_ref, *, kv_lora):
    x = x_ref[...]
    qa = jnp.dot(x, wq_ref[...], preferred_element_type=f32)
    qn_ref[...] = _rms(qa, qw_ref[...], RMS_EPS).astype(bf16)
    kv = jnp.dot(x, wkv_ref[...], preferred_element_type=f32)
    kvc_ref[...] = _rms(kv[:, :kv_lora], kw_ref[...], RMS_EPS).astype(bf16)
    kpe_ref[...] = _rope(kv[:, kv_lora:], cos_ref[...], sin_ref[...]).astype(bf16)


def mla_down(h, wq_a_b, wkv_a_b, q_norm_w, kv_norm_w, cos_k, sin_k, S, tm=512):
    T, D = h.shape
    QL = wq_a_b.shape[1]
    KVW = wkv_a_b.shape[1]
    KL = kv_norm_w.shape[0]
    nsb = S // tm
    full = lambda a: pl.BlockSpec(a.shape, lambda i: (0, 0))
    return pl.pallas_call(
        functools.partial(_mla_a_kernel, kv_lora=KL), grid=(T // tm,),
        in_specs=[pl.BlockSpec((tm, D), lambda i: (i, 0)), full(wq_a_b), full(wkv_a_b),
                  pl.BlockSpec((1, QL), lambda i: (0, 0)), pl.BlockSpec((1, KL), lambda i: (0, 0)),
                  pl.BlockSpec((tm, KVW - KL), lambda i: (i % nsb, 0)),
                  pl.BlockSpec((tm, KVW - KL), lambda i: (i % nsb, 0))],
        out_specs=[pl.BlockSpec((tm, QL), lambda i: (i, 0)), pl.BlockSpec((tm, KL), lambda i: (i, 0)),
                   pl.BlockSpec((tm, KVW - KL), lambda i: (i, 0))],
        out_shape=[jax.ShapeDtypeStruct((T, QL), bf16), jax.ShapeDtypeStruct((T, KL), bf16),
                   jax.ShapeDtypeStruct((T, KVW - KL), bf16)],
        compiler_params=_cparams(1), name="mla_down",
    )(h, wq_a_b, wkv_a_b, q_norm_w.reshape(1, QL), kv_norm_w.reshape(1, KL), cos_k, sin_k)


def _mla_attn_kernel(qn_ref, kvc_ref, kpe_ref, wq_ref, wkv_ref, cos_ref, sin_ref, o_ref, q_s, k_s, v_s, *, tq, c_exp):
    S = qn_ref.shape[0]
    nt = (((1,), (1,)), ((), ()))
    q = jnp.dot(qn_ref[...], wq_ref[...], preferred_element_type=f32)
    q_s[...] = _rope(q, cos_ref[...], sin_ref[...]).astype(bf16)
    kv = jnp.dot(kvc_ref[...], wkv_ref[...], preferred_element_type=f32)
    k_s[:, :QK_NOPE] = kv[:, :QK_NOPE].astype(bf16)
    k_s[:, QK_NOPE:] = kpe_ref[...]
    v_s[...] = kv[:, QK_NOPE:].astype(bf16)
    rows = lax.broadcasted_iota(jnp.int32, (tq, tq), 0)
    cols = lax.broadcasted_iota(jnp.int32, (tq, tq), 1)
    for i in range(S // tq):
        lo = i * tq
        qi = q_s[lo:lo + tq, :]
        s_d = lax.dot_general(qi, k_s[lo:lo + tq, :], nt, preferred_element_type=f32)
        s_d = jnp.where(rows >= cols, s_d, -jnp.inf)
        m = jnp.max(s_d, axis=-1, keepdims=True)
        if i > 0:
            s_o = lax.dot_general(qi, k_s[0:lo, :], nt, preferred_element_type=f32)
            m = jnp.maximum(m, jnp.max(s_o, axis=-1, keepdims=True))
        p_d = jnp.exp2((s_d - m) * c_exp)
        l = jnp.sum(p_d, axis=-1, keepdims=True)
        acc = jnp.dot(p_d.astype(bf16), v_s[lo:lo + tq, :], preferred_element_type=f32)
        if i > 0:
            p_o = jnp.exp2((s_o - m) * c_exp)
            l = l + jnp.sum(p_o, axis=-1, keepdims=True)
            acc = acc + jnp.dot(p_o.astype(bf16), v_s[0:lo, :], preferred_element_type=f32)
        o_ref[lo:lo + tq, :] = (acc / l).astype(o_ref.dtype)


def mla_attention(qn, kvc, kpe, wq_heads, wkv_heads, cos_q, sin_q, scale):
    B, S, QL = qn.shape
    KL = kvc.shape[2]
    H, _, W = wq_heads.shape
    tq = min(512, S)
    per_b = lambda w: pl.BlockSpec((None, S, w), lambda b, h: (b, 0, 0))
    per_h = lambda r, w: pl.BlockSpec((None, r, w), lambda b, h: (h, 0, 0))
    table = pl.BlockSpec((S, W), lambda b, h: (0, 0))
    return pl.pallas_call(
        functools.partial(_mla_attn_kernel, tq=tq, c_exp=scale * math.log2(math.e)), grid=(B, H),
        in_specs=[per_b(QL), per_b(KL), per_b(LANES), per_h(QL, W), per_h(KL, QK_NOPE + V_DIM), table, table],
        out_specs=pl.BlockSpec((None, S, V_DIM), lambda b, h: (b, 0, h)),
        out_shape=jax.ShapeDtypeStruct((B, S, H * V_DIM), bf16),
        scratch_shapes=[pltpu.VMEM((S, W), bf16), pltpu.VMEM((S, W), bf16), pltpu.VMEM((S, V_DIM), bf16)],
        compiler_params=_cparams(2), name="mla_attention",
    )(qn, kvc, kpe, wq_heads, wkv_heads, cos_q, sin_q)


def _rope_tables(S, width, rope0):
    half = QK_ROPE // 2
    inv_freq = ROPE_THETA ** (-jnp.arange(half, dtype=f32) * 2.0 / QK_ROPE)
    ang = jnp.arange(S).astype(f32)[:, None] * inv_freq[None, :]
    cos = jnp.repeat(jnp.cos(ang), 2, axis=1)
    sin = jnp.repeat(jnp.sin(ang), 2, axis=1) * jnp.tile(jnp.array([-1.0, 1.0], f32), half)[None, :]
    cos_t = jnp.ones((S, width), f32).at[:, rope0:rope0 + QK_ROPE].set(cos)
    sin_t = jnp.zeros((S, width), f32).at[:, rope0:rope0 + QK_ROPE].set(sin)
    return cos_t, sin_t


def mla_mixer(h, wq_a, q_norm_w, wq_b, wkv_a, kv_norm_w, wkv_b, wo, B, S):
    T, D = h.shape
    H = MLA_HEADS
    KL = kv_norm_w.shape[0]
    QL = wq_a.shape[1]
    W = 2 * LANES
    wkv_a_b = jnp.zeros((D, KL + LANES), f32).at[:, :KL + QK_ROPE].set(wkv_a).astype(bf16)
    wq_heads = jnp.zeros((H, QL, W), f32).at[:, :, :QK_NOPE + QK_ROPE].set(
        wq_b.reshape(QL, H, QK_NOPE + QK_ROPE).transpose(1, 0, 2)).astype(bf16)
    wkv_heads = wkv_b.reshape(KL, H, QK_NOPE + V_DIM).transpose(1, 0, 2).astype(bf16)
    cos_k, sin_k = _rope_tables(S, LANES, 0)
    cos_q, sin_q = _rope_tables(S, W, QK_NOPE)
    qn, kvc, kpe = mla_down(h, wq_a.astype(bf16), wkv_a_b, q_norm_w, kv_norm_w, cos_k, sin_k, S)
    o = mla_attention(qn.reshape(B, S, QL), kvc.reshape(B, S, KL), kpe.reshape(B, S, LANES), wq_heads, wkv_heads,
                      cos_q, sin_q, (QK_NOPE + QK_ROPE) ** -0.5)
    return matmul([o.reshape(T, H * V_DIM)], [wo.astype(bf16)], f32, 1024, 512, "attn_out_proj")


def ssd_pool_mixer(h, in_proj, conv_w, conv_b, dt_bias, a_log, d_skip, norm_w, pool_w, pool_scale, out_proj, B, S):
    T, D = h.shape
    DS = norm_w.shape[0]
    NH = dt_bias.shape[0]
    GN = SSM_GROUPS * SSM_STATE
    o1 = DS + DS + 2 * GN
    o2 = o1 + NH
    w_zx = in_proj[:, :o1].astype(bf16)
    w_dt = jnp.zeros((D, LANES), f32).at[:, :NH].set(in_proj[:, o1:o2]).astype(bf16)
    w_u = in_proj[:, o2:].astype(bf16)
    zx = matmul([h], [w_zx], f32, 1024, 512, "in_proj_zx").reshape(B, S, o1)
    dt_raw = matmul([h], [w_dt], f32, 1024, LANES, "in_proj_dt").reshape(B, S, LANES)
    u = matmul([h], [w_u], f32, 1024, 512, "in_proj_u").reshape(B, S, -1)
    xs = conv_silu(zx, DS, DS, conv_w, conv_b, 0, f32)
    bc = conv_silu(zx, 2 * DS, 2 * GN, conv_w, conv_b, DS, bf16)
    pad = lambda v: jnp.zeros((1, LANES), f32).at[0, :NH].set(v)
    a_row = pad(-jnp.exp(a_log))
    dskip_row = jnp.repeat(d_skip, SSM_HEAD_DIM).reshape(1, DS)
    y = ssd_mixer(xs, zx, dt_raw, bc, a_row, pad(dt_bias), dskip_row, norm_w)
    y_pool = pool_mixer(u, pool_w.astype(bf16), pool_scale)
    w_out = out_proj.astype(bf16)
    return matmul([y.reshape(T, DS), y_pool.reshape(T, -1)], [w_out[:DS], w_out[DS:]], f32, 1024, 256, "mixer_out_proj")


def kernel(x, c, ada_w, ada_b, ln_w, ln_b, in_proj, conv_w, conv_b, dt_bias, a_log, d_skip, ssd_norm_w, pool_w, pool_scale, out_proj, wq_a, q_norm_w, wq_b, wkv_a, kv_norm_w, wkv_b, wo, router_w, router_b, w_gu, b_gu, w_down, b_down):
    B, S, D = x.shape
    T = B * S
    mod = ada_modulation(c, ada_w, ada_b)
    m = lambda l, k: mod[l, :, k].reshape(B, 1, D)
    h = modulate(x, m(0, 1), m(0, 0))
    for i in range(DEPTH):
        j = i // 2
        hf = h.reshape(T, D)
        if i % 2 == 0:
            y = ssd_pool_mixer(hf, in_proj[j], conv_w[j], conv_b[j], dt_bias[j], a_log[j], d_skip[j],
                               ssd_norm_w[j], pool_w[j], pool_scale[j], out_proj[j], B, S)
        else:
            y = mla_mixer(hf, wq_a[j], q_norm_w[j], wq_b[j], wkv_a[j], kv_norm_w[j], wkv_b[j], wo[j], B, S)
        x, hff, logits = post_ln(x, [y.reshape(B, S, D)], m(i, 2), ln_w[i, 0], ln_b[i, 0], m(i, 4), m(i, 3),
                                 router_w[i], router_b[i])
        ys = moe_ffn(hff.reshape(T, D), logits.reshape(T, LANES), w_gu[i], b_gu[i], w_down[i], b_down[i])
        ys = [y.reshape(B, S, D) for y in ys]
        if i + 1 < DEPTH:
            x, h = post_ln(x, ys, m(i, 5), ln_w[i, 1], ln_b[i, 1], m(i + 1, 1), m(i + 1, 0))
        else:
            (x,) = post_ln(x, ys, m(i, 5), ln_w[i, 1], ln_b[i, 1])
    return x
```

```python
import functools
import math

import jax
import jax.numpy as jnp
from jax import lax
from jax.experimental import pallas as pl
from jax.experimental.pallas import tpu as pltpu

f32 = jnp.float32
bf16 = jnp.bfloat16

DEPTH = 2
DN_ALPHA = (2 * DEPTH) ** 0.25
LN_EPS = 1e-5
RMS_EPS = 1e-6
SSM_HEAD_DIM = 64
SSM_GROUPS = 8
SSM_STATE = 128
SSM_CHUNK = 128
CONV_K = 4
POOL_WINDOWS = (2, 4, 8, 16)
MLA_HEADS = 32
QK_NOPE = 128
QK_ROPE = 64
V_DIM = 128
ROPE_THETA = 10000.0
N_EXPERTS = 32
TOP_K = 4
SWIGLU_LIMIT = 7.0
SWIGLU_ALPHA = 1.702

LANES = 128
SUBLANES = 8
VMEM_LIMIT_BYTES = 56 * 1024 * 1024

MOE_TILE = 256


def _cparams(n_axes):
    return pltpu.CompilerParams(dimension_semantics=("arbitrary",) * n_axes, vmem_limit_bytes=VMEM_LIMIT_BYTES)


def _silu(v):
    return v * jax.nn.sigmoid(v)


def _ada_kernel(c_ref, w_ref, b_ref, o_ref):
    cond = _silu(c_ref[...])
    o_ref[...] = jnp.dot(cond.astype(bf16), w_ref[...].astype(bf16), preferred_element_type=f32) + b_ref[...]


def ada_modulation(c, ada_w, ada_b):
    B, D = c.shape
    L, _, N = ada_w.shape
    tn = 512
    cp = jnp.zeros((SUBLANES, D), f32).at[:B].set(c)
    out = pl.pallas_call(
        _ada_kernel,
        grid=(L, N // tn),
        in_specs=[pl.BlockSpec((SUBLANES, D), lambda l, j: (0, 0)),
                  pl.BlockSpec((None, D, tn), lambda l, j: (l, 0, j)),
                  pl.BlockSpec((None, 1, tn), lambda l, j: (l, 0, j))],
        out_specs=pl.BlockSpec((None, SUBLANES, tn), lambda l, j: (l, 0, j)),
        out_shape=jax.ShapeDtypeStruct((L, SUBLANES, N), f32),
        compiler_params=_cparams(2), name="ada_modulation",
    )(cp, ada_w, ada_b.reshape(L, 1, N))
    return out[:, :B].reshape(L, B, 6, D)


def _modulate_kernel(x_ref, sc_ref, sh_ref, o_ref):
    o_ref[...] = (x_ref[...] * (1.0 + sc_ref[...]) + sh_ref[...]).astype(o_ref.dtype)


def modulate(x, sc, sh, ts=512):
    B, S, D = x.shape
    row = pl.BlockSpec((None, ts, D), lambda b, i: (b, i, 0))
    per_b = pl.BlockSpec((None, 1, D), lambda b, i: (b, 0, 0))
    return pl.pallas_call(
        _modulate_kernel, grid=(B, S // ts), in_specs=[row, per_b, per_b], out_specs=row,
        out_shape=jax.ShapeDtypeStruct((B, S, D), bf16), compiler_params=_cparams(2), name="modulate",
    )(x, sc, sh)


def _mm_kernel(*refs, n_pairs):
    o_ref = refs[-1]
    acc = None
    for p in range(n_pairs):
        d = jnp.dot(refs[p][...], refs[n_pairs + p][...].astype(bf16), preferred_element_type=f32)
        acc = d if acc is None else acc + d
    o_ref[...] = acc.astype(o_ref.dtype)


def matmul(xs, ws, out_dtype, tm, tn, name, N=None, w_blocks=None):
    M = xs[0].shape[0]
    N = ws[0].shape[1] if N is None else N
    w_blocks = [(0, 0)] * len(ws) if w_blocks is None else w_blocks
    tm = min(tm, M)
    assert M % tm == 0 and N % tn == 0
    in_specs = [pl.BlockSpec((tm, x.shape[1]), lambda i, j: (i, 0)) for x in xs]
    in_specs += [pl.BlockSpec((x.shape[1], tn), lambda i, j, rb=rb, cb=cb: (rb, cb + j))
                 for x, (rb, cb) in zip(xs, w_blocks)]
    return pl.pallas_call(
        functools.partial(_mm_kernel, n_pairs=len(xs)),
        grid=(M // tm, N // tn), in_specs=in_specs,
        out_specs=pl.BlockSpec((tm, tn), lambda i, j: (i, j)),
        out_shape=jax.ShapeDtypeStruct((M, N), out_dtype), compiler_params=_cparams(2), name=name,
    )(*xs, *ws)


def _conv_kernel(cur_ref, halo_ref, w_ref, b_ref, o_ref, buf_ref):
    ts = cur_ref.shape[0]
    cur = cur_ref[...]
    buf_ref[0:SUBLANES, :] = jnp.where(pl.program_id(1) == 0, 0.0, halo_ref[...])
    buf_ref[SUBLANES:SUBLANES + ts, :] = cur
    acc = b_ref[...] + w_ref[CONV_K - 1:CONV_K, :] * cur
    for k in range(CONV_K - 1):
        acc += w_ref[k:k + 1, :] * buf_ref[pl.ds(SUBLANES - (CONV_K - 1 - k), ts), :]
    o_ref[...] = _silu(acc).astype(o_ref.dtype)


def conv_silu(src, col0, n_ch, conv_w, conv_b, ch0, out_dtype, ts=512, tc=512):
    B, S, _ = src.shape
    cb0, wb0 = col0 // tc, ch0 // tc
    hb = ts // SUBLANES
    return pl.pallas_call(
        _conv_kernel, grid=(B, S // ts, n_ch // tc),
        in_specs=[pl.BlockSpec((None, ts, tc), lambda b, i, j: (b, i, cb0 + j)),
                  pl.BlockSpec((None, SUBLANES, tc), lambda b, i, j: (b, jnp.maximum(i * hb - 1, 0), cb0 + j)),
                  pl.BlockSpec((CONV_K, tc), lambda b, i, j: (0, wb0 + j)),
                  pl.BlockSpec((1, tc), lambda b, i, j: (0, wb0 + j))],
        out_specs=pl.BlockSpec((None, ts, tc), lambda b, i, j: (b, i, j)),
        out_shape=jax.ShapeDtypeStruct((B, S, n_ch), out_dtype),
        scratch_shapes=[pltpu.VMEM((ts + SUBLANES, tc), f32)],
        compiler_params=_cparams(3), name="conv_silu",
    )(src, src, conv_w, conv_b.reshape(1, -1))


def _ssd_kernel(xs_ref, z_ref, dt_ref, bc_ref, a_ref, dtb_ref, dskip_ref, nw_ref, o_ref, state_ref, y_ref):
    L = SSM_CHUNK
    P = SSM_HEAD_DIM
    G = SSM_GROUPS
    GW = xs_ref.shape[1] // G
    NS = SSM_STATE

    @pl.when(pl.program_id(1) == 0)
    def _():
        state_ref[...] = jnp.zeros_like(state_ref)

    raw = dt_ref[...] + dtb_ref[...]
    dt = jnp.maximum(raw, 0.0) + jnp.log1p(jnp.exp(-jnp.abs(raw)))
    la = dt * a_ref[...]
    row = lax.broadcasted_iota(jnp.int32, (L, LANES), 0)
    cum = la
    k = 1
    while k < L:
        cum = cum + jnp.where(row >= k, pltpu.roll(cum, k, axis=0), 0.0)
        k *= 2
    cum_t = cum.T
    last = cum[L - 1:L, :]
    to_end = jnp.exp(last - cum)
    e_cum = jnp.exp(cum)
    c_dec = jnp.exp(last)
    causal = row >= lax.broadcasted_iota(jnp.int32, (L, L), 1)
    left = lax.broadcasted_iota(jnp.int32, (L, LANES), 1) < P
    left1 = left[0:1, :]

    for g in range(G):
        bm = bc_ref[:, g * NS:(g + 1) * NS]
        cm = bc_ref[:, G * NS + g * NS:G * NS + (g + 1) * NS]
        cb = lax.dot_general(cm, bm, (((1,), (1,)), ((), ())), preferred_element_type=f32)
        h_in = state_ref[g]
        y_off = jnp.dot(cm, h_in.astype(bf16), preferred_element_type=f32)
        xw_parts = []
        cd_parts = []
        for j in range(GW // LANES):
            h0 = (g * GW) // P + 2 * j
            c0 = g * GW + j * LANES

            def pair(m, h0=h0):
                return jnp.where(left, m[:, h0:h0 + 1], m[:, h0 + 1:h0 + 2])

            xs = xs_ref[:, c0:c0 + LANES]
            xdt = xs * pair(dt)
            xdt_b = xdt.astype(bf16)
            dec = []
            for h in (h0, h0 + 1):
                seg = cum[:, h:h + 1] - cum_t[h:h + 1, :]
                dec.append((cb * jnp.exp(jnp.where(causal, seg, -jnp.inf))).astype(bf16))
            m_cat = jnp.concatenate(dec, axis=1)
            zero = jnp.zeros_like(xdt_b)
            x_blk = jnp.concatenate([jnp.where(left, xdt_b, zero), jnp.where(left, zero, xdt_b)], axis=0)
            y_diag = jnp.dot(m_cat, x_blk, preferred_element_type=f32)
            y = y_diag + y_off[:, j * LANES:(j + 1) * LANES] * pair(e_cum) + dskip_ref[:, c0:c0 + LANES] * xs
            y_ref[:, c0:c0 + LANES] = y
            xw_parts.append((xdt * pair(to_end)).astype(bf16))
            cd_parts.append(jnp.where(left1, c_dec[:, h0:h0 + 1], c_dec[:, h0 + 1:h0 + 2]))
        xw = jnp.concatenate(xw_parts, axis=1)
        bm_t = bm.astype(f32).T.astype(bf16)
        st = jnp.dot(bm_t, xw, preferred_element_type=f32)
        state_ref[g] = h_in * jnp.concatenate(cd_parts, axis=1) + st

    for g in range(G):
        sl = slice(g * GW, (g + 1) * GW)
        gz = y_ref[:, sl] * _silu(z_ref[:, sl])
        ms = jnp.mean(gz * gz, axis=-1, keepdims=True)
        o_ref[:, sl] = (gz * lax.rsqrt(ms + LN_EPS) * nw_ref[:, sl]).astype(o_ref.dtype)


def ssd_mixer(xs, zx, dt_raw, bc, a_row, dtb_row, dskip_row, norm_w):
    B, S, DS = xs.shape
    L = SSM_CHUNK
    GW = DS // SSM_GROUPS
    blk = lambda w: pl.BlockSpec((None, L, w), lambda b, c: (b, c, 0))
    vec = lambda w: pl.BlockSpec((1, w), lambda b, c: (0, 0))
    return pl.pallas_call(
        _ssd_kernel, grid=(B, S // L),
        in_specs=[blk(DS), blk(DS), blk(LANES), blk(bc.shape[2]), vec(LANES), vec(LANES), vec(DS), vec(DS)],
        out_specs=blk(DS),
        out_shape=jax.ShapeDtypeStruct((B, S, DS), bf16),
        scratch_shapes=[pltpu.VMEM((SSM_GROUPS, SSM_STATE, GW), f32), pltpu.VMEM((L, DS), f32)],
        compiler_params=_cparams(2), name="ssd_mixer",
    )(xs, zx, dt_raw, bc, a_row, dtb_row, dskip_row, norm_w.reshape(1, DS))


def _pool_kernel(u_ref, halo_ref, w_ref, sc_ref, o_ref, *, halo_rows):
    g = pl.program_id(0)
    i = pl.program_id(2)
    ts = u_ref.shape[0]
    win = jnp.int32(POOL_WINDOWS[-1])
    for gi in range(len(POOL_WINDOWS) - 2, -1, -1):
        win = jnp.where(g == gi, jnp.int32(POOL_WINDOWS[gi]), win)
    u = u_ref[...]
    halo = jnp.where(i == 0, 0.0, halo_ref[...])
    s = jnp.concatenate([halo, u], axis=0)
    k = 1
    while k < POOL_WINDOWS[-1]:
        s = jnp.where(win > k, s + pltpu.roll(s, k, axis=0), s)
        k *= 2
    s = s[halo_rows:, :]
    t1 = i * ts + 1 + lax.broadcasted_iota(jnp.int32, (ts, 1), 0)
    cnt = jnp.minimum(t1, win).astype(f32)
    diff = (s / cnt - u).astype(bf16)
    o_ref[...] = (jnp.dot(diff, w_ref[...], preferred_element_type=f32) * sc_ref[...]).astype(o_ref.dtype)


def pool_mixer(u, pool_w_b, pool_scale, ts=512):
    B, S, DP = u.shape
    NG, GW, _ = pool_w_b.shape
    halo_rows = POOL_WINDOWS[-1]
    hb = ts // halo_rows
    return pl.pallas_call(
        functools.partial(_pool_kernel, halo_rows=halo_rows), grid=(NG, B, S // ts),
        in_specs=[pl.BlockSpec((None, ts, GW), lambda g, b, i: (b, i, g)),
                  pl.BlockSpec((None, halo_rows, GW), lambda g, b, i: (b, jnp.maximum(i * hb - 1, 0), g)),
                  pl.BlockSpec((None, GW, GW), lambda g, b, i: (g, 0, 0)),
                  pl.BlockSpec((1, GW), lambda g, b, i: (0, g))],
        out_specs=pl.BlockSpec((None, ts, GW), lambda g, b, i: (b, i, g)),
        out_shape=jax.ShapeDtypeStruct((B, S, DP), bf16),
        compiler_params=_cparams(3), name="pool_mixer",
    )(u, u, pool_w_b, pool_scale.reshape(1, DP))


def _ln_kernel(*refs, n_y, with_h, with_router):
    x_ref = refs[0]
    y_refs = refs[1:1 + n_y]
    g_ref, w_ref, b_ref = refs[1 + n_y:4 + n_y]
    pos = 4 + n_y
    y = y_refs[0][...].astype(f32)
    for r in y_refs[1:]:
        y = y + r[...].astype(f32)
    v = DN_ALPHA * x_ref[...] + (1.0 + g_ref[...]) * y
    mu = jnp.mean(v, axis=-1, keepdims=True)
    d = v - mu
    var = jnp.mean(d * d, axis=-1, keepdims=True)
    xn = d * lax.rsqrt(var + LN_EPS) * w_ref[...] + b_ref[...]
    if with_h:
        sc_ref, sh_ref = refs[pos:pos + 2]
        pos += 2
    if with_router:
        rw_ref, rb_ref = refs[pos:pos + 2]
        pos += 2
    xo_ref = refs[pos]
    xo_ref[...] = xn
    if with_h:
        h = xn * (1.0 + sc_ref[...]) + sh_ref[...]
        refs[pos + 1][...] = h.astype(bf16)
        if with_router:
            refs[pos + 2][...] = jnp.dot(h, rw_ref[...], preferred_element_type=f32,
                                         precision=lax.Precision.HIGHEST) + rb_ref[...]


def post_ln(x, ys, gate, ln_w, ln_b, sc=None, sh=None, router_w=None, router_b=None, ts=256):
    B, S, D = x.shape
    with_h = sc is not None
    with_router = router_w is not None
    row = pl.BlockSpec((None, ts, D), lambda b, i: (b, i, 0))
    per_b = pl.BlockSpec((None, 1, D), lambda b, i: (b, 0, 0))
    vec = pl.BlockSpec((1, D), lambda b, i: (0, 0))
    args = [x, *ys, gate, ln_w.reshape(1, D), ln_b.reshape(1, D)]
    in_specs = [row] * (1 + len(ys)) + [per_b, vec, vec]
    out_shape = [jax.ShapeDtypeStruct((B, S, D), f32)]
    out_specs = [row]
    if with_h:
        args += [sc, sh]
        in_specs += [per_b, per_b]
        out_shape.append(jax.ShapeDtypeStruct((B, S, D), bf16))
        out_specs.append(row)
    if with_router:
        NE = router_w.shape[1]
        rw = jnp.zeros((D, LANES), f32).at[:, :NE].set(router_w)
        rb = jnp.zeros((1, LANES), f32).at[0, :NE].set(router_b)
        args += [rw, rb]
        in_specs += [pl.BlockSpec((D, LANES), lambda b, i: (0, 0)), pl.BlockSpec((1, LANES), lambda b, i: (0, 0))]
        out_shape.append(jax.ShapeDtypeStruct((B, S, LANES), f32))
        out_specs.append(pl.BlockSpec((None, ts, LANES), lambda b, i: (b, i, 0)))
    return pl.pallas_call(
        functools.partial(_ln_kernel, n_y=len(ys), with_h=with_h, with_router=with_router),
        grid=(B, S // ts), in_specs=in_specs, out_specs=out_specs, out_shape=out_shape,
        compiler_params=_cparams(2), name="post_ln",
    )(*args)


def _cache_expert_weight(te_ref, w_ref, wbf_ref, rows_per_step):
    i = pl.program_id(0)
    changed = jnp.logical_or(i == 0, te_ref[i] != te_ref[jnp.maximum(i - 1, 0)])

    @pl.when(changed)
    def _():
        def body(r, carry):
            r0 = pl.multiple_of(r * rows_per_step, rows_per_step)
            wbf_ref[pl.ds(r0, rows_per_step), :] = w_ref[pl.ds(r0, rows_per_step), :].astype(bf16)
            return carry

        lax.fori_loop(0, w_ref.shape[0] // rows_per_step, body, 0)


def _expert_gu_kernel(te_ref, nu_ref, x_ref, wgu_ref, bgu_ref, sel_ref, h_ref, wbf_ref):
    i = pl.program_id(0)
    _cache_expert_weight(te_ref, wgu_ref, wbf_ref, 256)

    @pl.when(i < nu_ref[0])
    def _():
        gu = jnp.dot(x_ref[...], wbf_ref[...], preferred_element_type=f32) + bgu_ref[...]
        gate = jnp.minimum(gu, SWIGLU_LIMIT)
        up = jnp.clip(gu, -SWIGLU_LIMIT, SWIGLU_LIMIT)
        glu = gate * jax.nn.sigmoid(SWIGLU_ALPHA * gate)
        up_next = pltpu.roll(up, up.shape[1] - 1, axis=1)
        hf = ((up_next + 1.0) * glu).astype(bf16)
        h_ref[...] = jnp.dot(hf, sel_ref[...], preferred_element_type=f32).astype(bf16)

    @pl.when(i >= nu_ref[0])
    def _():
        h_ref[...] = jnp.zeros_like(h_ref)


def _expert_down_kernel(te_ref, nu_ref, h_ref, p_ref, wd_ref, bd_ref, o_ref, wbf_ref):
    i = pl.program_id(0)
    _cache_expert_weight(te_ref, wd_ref, wbf_ref, 64)

    @pl.when(i < nu_ref[0])
    def _():
        out = jnp.dot(h_ref[...], wbf_ref[...], preferred_element_type=f32) + bd_ref[...]
        o_ref[...] = (out * p_ref[...]).astype(o_ref.dtype)

    @pl.when(i >= nu_ref[0])
    def _():
        o_ref[...] = jnp.zeros_like(o_ref)


def expert_ffn(tile_expert, n_used, xe, row_p, w_gu, bgu, sel, w_down, bd):
    R, D = xe.shape
    E, _, F2 = w_gu.shape
    F = F2 // 2
    TM = MOE_TILE
    e_blk = lambda r, c: pl.BlockSpec((None, r, c), lambda i, te, nu: (te[i], 0, 0))
    tile = lambda c: pl.BlockSpec((TM, c), lambda i, te, nu: (i, 0))
    h = pl.pallas_call(
        _expert_gu_kernel,
        grid_spec=pltpu.PrefetchScalarGridSpec(
            num_scalar_prefetch=2, grid=(R // TM,),
            in_specs=[tile(D), e_blk(D, F2), e_blk(1, F2), pl.BlockSpec((F2, F), lambda i, te, nu: (0, 0))],
            out_specs=tile(F), scratch_shapes=[pltpu.VMEM((D, F2), bf16)]),
        out_shape=jax.ShapeDtypeStruct((R, F), bf16), compiler_params=_cparams(1), name="expert_gate_up",
    )(tile_expert, n_used, xe, w_gu, bgu, sel)
    return pl.pallas_call(
        _expert_down_kernel,
        grid_spec=pltpu.PrefetchScalarGridSpec(
            num_scalar_prefetch=2, grid=(R // TM,),
            in_specs=[tile(F), tile(1), e_blk(F, D), e_blk(1, D)],
            out_specs=tile(D), scratch_shapes=[pltpu.VMEM((F, D), bf16)]),
        out_shape=jax.ShapeDtypeStruct((R, D), bf16), compiler_params=_cparams(1), name="expert_down",
    )(tile_expert, n_used, h, row_p, w_down, bd)


def moe_ffn(h, logits, w_gu, b_gu, w_down, b_down):
    T, D = h.shape
    E, TM = N_EXPERTS, MOE_TILE
    F2 = w_gu.shape[2]
    top_v, top_e = lax.top_k(logits[:, :E], TOP_K)
    top_p = jax.nn.softmax(top_v, axis=-1)
    A = T * TOP_K
    flat_e = top_e.reshape(A)
    order = jnp.argsort(flat_e, stable=True).astype(jnp.int32)
    pos = jnp.argsort(order).astype(jnp.int32)
    seg_start = jnp.searchsorted(flat_e[order], jnp.arange(E, dtype=flat_e.dtype), side="left").astype(jnp.int32)
    counts = jnp.diff(jnp.concatenate([seg_start, jnp.array([A], jnp.int32)]))
    ntile_e = (counts + TM - 1) // TM
    tile_end = jnp.cumsum(ntile_e)
    pad_start = (tile_end - ntile_e) * TM
    dest = pad_start[flat_e] + pos - seg_start[flat_e]
    n_tiles = (A + E * (TM - 1)) // TM
    R = n_tiles * TM
    n_used = tile_end[-1:].astype(jnp.int32)
    tile_expert = jnp.minimum(jnp.searchsorted(tile_end, jnp.arange(n_tiles), side="right"), E - 1).astype(jnp.int32)
    tile_expert = jnp.where(jnp.arange(n_tiles) < n_used[0], tile_expert, tile_expert[n_used[0] - 1])
    row = jnp.arange(R, dtype=jnp.int32)
    row_e = jnp.repeat(tile_expert, TM)
    src = seg_start[row_e] + row - pad_start[row_e]
    valid = src < seg_start[row_e] + counts[row_e]
    asg = order[jnp.clip(src, 0, A - 1)]
    row_tok = jnp.where(valid, asg // TOP_K, 0)
    row_p = jnp.where(valid, top_p.reshape(A)[asg], 0.0)
    xe = h[row_tok]
    sel = (jnp.arange(F2)[:, None] == 2 * jnp.arange(F2 // 2)[None, :]).astype(bf16)
    out = expert_ffn(tile_expert, n_used, xe, row_p.reshape(R, 1), w_gu, b_gu[:, None, :], sel,
                     w_down, b_down[:, None, :])
    dest_k = dest.reshape(T, TOP_K)
    return [out[dest_k[:, k]] for k in range(TOP_K)]


def _rope(v, cos, sin):
    n = v.shape[-1]
    even = lax.broadcasted_iota(jnp.int32, v.shape, 1) % 2 == 0
    partner = jnp.where(even, pltpu.roll(v, n - 1, axis=1), pltpu.roll(v, 1, axis=1))
    return v * cos + partner * sin


def _rms(v, w, eps):
    return v * lax.rsqrt(jnp.mean(v * v, axis=-1, keepdims=True) + eps) * w


def _mla_a_kernel(x_ref, wq_ref, wkv_ref, qw_ref, kw_ref, cos_ref, sin_ref, qn_ref, kvc_ref, kpe_ref, *, kv_lora):
    x = x_ref[...]
    qa = jnp.dot(x, wq_ref[...], preferred_element_type=f32)
    qn_ref[...] = _rms(qa, qw_ref[...], RMS_EPS).astype(bf16)
    kv = jnp.dot(x, wkv_ref[...], preferred_element_type=f32)
    kvc_ref[...] = _rms(kv[:, :kv_lora], kw_ref[...], RMS_EPS).astype(bf16)
    kpe_ref[...] = _rope(kv[:, kv_lora:], cos_ref[...], sin_ref[...]).astype(bf16)


def mla_down(h, wq_a_b, wkv_a_b, q_norm_w, kv_norm_w, cos_k, sin_k, S, tm=512):
    T, D = h.shape
    QL = wq_a_b.shape[1]
    KVW = wkv_a_b.shape[1]
    KL = kv_norm_w.shape[0]
    nsb = S // tm
    full = lambda a: pl.BlockSpec(a.shape, lambda i: (0, 0))
    return pl.pallas_call(
        functools.partial(_mla_a_kernel, kv_lora=KL), grid=(T // tm,),
        in_specs=[pl.BlockSpec((tm, D), lambda i: (i, 0)), full(wq_a_b), full(wkv_a_b),
                  pl.BlockSpec((1, QL), lambda i: (0, 0)), pl.BlockSpec((1, KL), lambda i: (0, 0)),
                  pl.BlockSpec((tm, KVW - KL), lambda i: (i % nsb, 0)),
                  pl.BlockSpec((tm, KVW - KL), lambda i: (i % nsb, 0))],
        out_specs=[pl.BlockSpec((tm, QL), lambda i: (i, 0)), pl.BlockSpec((tm, KL), lambda i: (i, 0)),
                   pl.BlockSpec((tm, KVW - KL), lambda i: (i, 0))],
        out_shape=[jax.ShapeDtypeStruct((T, QL), bf16), jax.ShapeDtypeStruct((T, KL), bf16),
                   jax.ShapeDtypeStruct((T, KVW - KL), bf16)],
        compiler_params=_cparams(1), name="mla_down",
    )(h, wq_a_b, wkv_a_b, q_norm_w.reshape(1, QL), kv_norm_w.reshape(1, KL), cos_k, sin_k)


def _mla_attn_kernel(qn_ref, kvc_ref, kpe_ref, wq_ref, wkv_ref, cos_ref, sin_ref, o_ref, q_s, k_s, v_s, *, tq, c_exp):
    S = qn_ref.shape[0]
    nt = (((1,), (1,)), ((), ()))
    q = jnp.dot(qn_ref[...], wq_ref[...], preferred_element_type=f32)
    q_s[...] = _rope(q, cos_ref[...], sin_ref[...]).astype(bf16)
    kv = jnp.dot(kvc_ref[...], wkv_ref[...], preferred_element_type=f32)
    k_s[:, :QK_NOPE] = kv[:, :QK_NOPE].astype(bf16)
    k_s[:, QK_NOPE:] = kpe_ref[...]
    v_s[...] = kv[:, QK_NOPE:].astype(bf16)
    rows = lax.broadcasted_iota(jnp.int32, (tq, tq), 0)
    cols = lax.broadcasted_iota(jnp.int32, (tq, tq), 1)
    for i in range(S // tq):
        lo = i * tq
        qi = q_s[lo:lo + tq, :]
        s_d = lax.dot_general(qi, k_s[lo:lo + tq, :], nt, preferred_element_type=f32)
        s_d = jnp.where(rows >= cols, s_d, -jnp.inf)
        m = jnp.max(s_d, axis=-1, keepdims=True)
        if i > 0:
            s_o = lax.dot_general(qi, k_s[0:lo, :], nt, preferred_element_type=f32)
            m = jnp.maximum(m, jnp.max(s_o, axis=-1, keepdims=True))
        p_d = jnp.exp2((s_d - m) * c_exp)
        l = jnp.sum(p_d, axis=-1, keepdims=True)
        acc = jnp.dot(p_d.astype(bf16), v_s[lo:lo + tq, :], preferred_element_type=f32)
        if i > 0:
            p_o = jnp.exp2((s_o - m) * c_exp)
            l = l + jnp.sum(p_o, axis=-1, keepdims=True)
            acc = acc + jnp.dot(p_o.astype(bf16), v_s[0:lo, :], preferred_element_type=f32)
        o_ref[lo:lo + tq, :] = (acc / l).astype(o_ref.dtype)


def mla_attention(qn, kvc, kpe, wq_heads, wkv_heads, cos_q, sin_q, scale):
    B, S, QL = qn.shape
    KL = kvc.shape[2]
    H, _, W = wq_heads.shape
    tq = min(512, S)
    per_b = lambda w: pl.BlockSpec((None, S, w), lambda b, h: (b, 0, 0))
    per_h = lambda r, w: pl.BlockSpec((None, r, w), lambda b, h: (h, 0, 0))
    table = pl.BlockSpec((S, W), lambda b, h: (0, 0))
    return pl.pallas_call(
        functools.partial(_mla_attn_kernel, tq=tq, c_exp=scale * math.log2(math.e)), grid=(B, H),
        in_specs=[per_b(QL), per_b(KL), per_b(LANES), per_h(QL, W), per_h(KL, QK_NOPE + V_DIM), table, table],
        out_specs=pl.BlockSpec((None, S, V_DIM), lambda b, h: (b, 0, h)),
        out_shape=jax.ShapeDtypeStruct((B, S, H * V_DIM), bf16),
        scratch_shapes=[pltpu.VMEM((S, W), bf16), pltpu.VMEM((S, W), bf16), pltpu.VMEM((S, V_DIM), bf16)],
        compiler_params=_cparams(2), name="mla_attention",
    )(qn, kvc, kpe, wq_heads, wkv_heads, cos_q, sin_q)


def _rope_tables(S, width, rope0):
    half = QK_ROPE // 2
    inv_freq = ROPE_THETA ** (-jnp.arange(half, dtype=f32) * 2.0 / QK_ROPE)
    ang = jnp.arange(S).astype(f32)[:, None] * inv_freq[None, :]
    cos = jnp.repeat(jnp.cos(ang), 2, axis=1)
    sin = jnp.repeat(jnp.sin(ang), 2, axis=1) * jnp.tile(jnp.array([-1.0, 1.0], f32), half)[None, :]
    cos_t = jnp.ones((S, width), f32).at[:, rope0:rope0 + QK_ROPE].set(cos)
    sin_t = jnp.zeros((S, width), f32).at[:, rope0:rope0 + QK_ROPE].set(sin)
    return cos_t, sin_t


def mla_mixer(h, wq_a, q_norm_w, wq_b, wkv_a, kv_norm_w, wkv_b, wo, B, S):
    T, D = h.shape
    H = MLA_HEADS
    KL = kv_norm_w.shape[0]
    QL = wq_a.shape[1]
    W = 2 * LANES
    wkv_a_b = jnp.zeros((D, KL + LANES), f32).at[:, :KL + QK_ROPE].set(wkv_a).astype(bf16)
    wq_heads = jnp.zeros((H, QL, W), f32).at[:, :, :QK_NOPE + QK_ROPE].set(
        wq_b.reshape(QL, H, QK_NOPE + QK_ROPE).transpose(1, 0, 2)).astype(bf16)
    wkv_heads = wkv_b.reshape(KL, H, QK_NOPE + V_DIM).transpose(1, 0, 2).astype(bf16)
    cos_k, sin_k = _rope_tables(S, LANES, 0)
    cos_q, sin_q = _rope_tables(S, W, QK_NOPE)
    qn, kvc, kpe = mla_down(h, wq_a.astype(bf16), wkv_a_b, q_norm_w, kv_norm_w, cos_k, sin_k, S)
    o = mla_attention(qn.reshape(B, S, QL), kvc.reshape(B, S, KL), kpe.reshape(B, S, LANES), wq_heads, wkv_heads,
                      cos_q, sin_q, (QK_NOPE + QK_ROPE) ** -0.5)
    return matmul([o.reshape(T, H * V_DIM)], [wo], f32, 1024, 512, "attn_out_proj")


def ssd_pool_mixer(h, in_proj, conv_w, conv_b, dt_bias, a_log, d_skip, norm_w, pool_w, pool_scale, out_proj, B, S):
    T, D = h.shape
    DS = norm_w.shape[0]
    NH = dt_bias.shape[0]
    GN = SSM_GROUPS * SSM_STATE
    o1 = DS + DS + 2 * GN
    o2 = o1 + NH
    assert o1 % LANES == 0
    zx = matmul([h], [in_proj], f32, 1024, 512, "in_proj_zx", N=o1).reshape(B, S, o1)
    dt_raw = matmul([h], [in_proj], f32, 1024, LANES, "in_proj_dt", N=LANES,
                    w_blocks=[(0, o1 // LANES)]).reshape(B, S, LANES)
    u = matmul([h], [in_proj[:, o2:]], f32, 1024, 512, "in_proj_u").reshape(B, S, -1)
    xs = conv_silu(zx, DS, DS, conv_w, conv_b, 0, f32)
    bc = conv_silu(zx, 2 * DS, 2 * GN, conv_w, conv_b, DS, bf16)
    pad = lambda v: jnp.zeros((1, LANES), f32).at[0, :NH].set(v)
    a_row = pad(-jnp.exp(a_log))
    dskip_row = jnp.repeat(d_skip, SSM_HEAD_DIM).reshape(1, DS)
    y = ssd_mixer(xs, zx, dt_raw, bc, a_row, pad(dt_bias), dskip_row, norm_w)
    y_pool = pool_mixer(u, pool_w.astype(bf16), pool_scale)
    assert y_pool.shape[2] == DS
    return matmul([y.reshape(T, DS), y_pool.reshape(T, DS)], [out_proj, out_proj], f32, 1024, 256, "mixer_out_proj",
                  w_blocks=[(0, 0), (1, 0)])


def kernel(x, c, ada_w, ada_b, ln_w, ln_b, in_proj, conv_w, conv_b, dt_bias, a_log, d_skip, ssd_norm_w, pool_w, pool_scale, out_proj, wq_a, q_norm_w, wq_b, wkv_a, kv_norm_w, wkv_b, wo, router_w, router_b, w_gu, b_gu, w_down, b_down):
    B, S, D = x.shape
    T = B * S
    mod = ada_modulation(c, ada_w, ada_b)
    m = lambda l, k: mod[l, :, k].reshape(B, 1, D)
    h = modulate(x, m(0, 1), m(0, 0))
    for i in range(DEPTH):
        j = i // 2
        hf = h.reshape(T, D)
        if i % 2 == 0:
            y = ssd_pool_mixer(hf, in_proj[j], conv_w[j], conv_b[j], dt_bias[j], a_log[j], d_skip[j],
                               ssd_norm_w[j], pool_w[j], pool_scale[j], out_proj[j], B, S)
        else:
            y = mla_mixer(hf, wq_a[j], q_norm_w[j], wq_b[j], wkv_a[j], kv_norm_w[j], wkv_b[j], wo[j], B, S)
        x, hff, logits = post_ln(x, [y.reshape(B, S, D)], m(i, 2), ln_w[i, 0], ln_b[i, 0], m(i, 4), m(i, 3),
                                 router_w[i], router_b[i])
        ys = moe_ffn(hff.reshape(T, D), logits.reshape(T, LANES), w_gu[i], b_gu[i], w_down[i], b_down[i])
        ys = [y.reshape(B, S, D) for y in ys]
        if i + 1 < DEPTH:
            x, h = post_ln(x, ys, m(i, 5), ln_w[i, 1], ln_b[i, 1], m(i + 1, 1), m(i + 1, 0))
        else:
            (x,) = post_ln(x, ys, m(i, 5), ln_w[i, 1], ln_b[i, 1])
    return x
```

```python
import functools
import math

import jax
import jax.numpy as jnp
from jax import lax
from jax.experimental import pallas as pl
from jax.experimental.pallas import tpu as pltpu

f32 = jnp.float32
bf16 = jnp.bfloat16

DEPTH = 2
DN_ALPHA = (2 * DEPTH) ** 0.25
LN_EPS = 1e-5
RMS_EPS = 1e-6
SSM_HEAD_DIM = 64
SSM_GROUPS = 8
SSM_STATE = 128
SSM_CHUNK = 128
CONV_K = 4
POOL_WINDOWS = (2, 4, 8, 16)
MLA_HEADS = 32
QK_NOPE = 128
QK_ROPE = 64
V_DIM = 128
ROPE_THETA = 10000.0
N_EXPERTS = 32
TOP_K = 4
SWIGLU_LIMIT = 7.0
SWIGLU_ALPHA = 1.702

LANES = 128
SUBLANES = 8
VMEM_LIMIT_BYTES = 56 * 1024 * 1024

MOE_TILE = 256


def _cparams(n_axes):
    return pltpu.CompilerParams(dimension_semantics=("arbitrary",) * n_axes, vmem_limit_bytes=VMEM_LIMIT_BYTES)


def _silu(v):
    return v * jax.nn.sigmoid(v)


def _ada_kernel(c_ref, w_ref, b_ref, o_ref):
    cond = _silu(c_ref[...])
    o_ref[...] = jnp.dot(cond.astype(bf16), w_ref[...].astype(bf16), preferred_element_type=f32) + b_ref[...]


def ada_modulation(c, ada_w, ada_b):
    B, D = c.shape
    L, _, N = ada_w.shape
    tn = 512
    cp = jnp.zeros((SUBLANES, D), f32).at[:B].set(c)
    out = pl.pallas_call(
        _ada_kernel,
        grid=(L, N // tn),
        in_specs=[pl.BlockSpec((SUBLANES, D), lambda l, j: (0, 0)),
                  pl.BlockSpec((None, D, tn), lambda l, j: (l, 0, j)),
                  pl.BlockSpec((None, 1, tn), lambda l, j: (l, 0, j))],
        out_specs=pl.BlockSpec((None, SUBLANES, tn), lambda l, j: (l, 0, j)),
        out_shape=jax.ShapeDtypeStruct((L, SUBLANES, N), f32),
        compiler_params=_cparams(2), name="ada_modulation",
    )(cp, ada_w, ada_b.reshape(L, 1, N))
    return out[:, :B].reshape(L, B, 6, D)


def _modulate_kernel(x_ref, sc_ref, sh_ref, o_ref):
    o_ref[...] = (x_ref[...] * (1.0 + sc_ref[...]) + sh_ref[...]).astype(o_ref.dtype)


def modulate(x, sc, sh, ts=512):
    B, S, D = x.shape
    row = pl.BlockSpec((None, ts, D), lambda b, i: (b, i, 0))
    per_b = pl.BlockSpec((None, 1, D), lambda b, i: (b, 0, 0))
    return pl.pallas_call(
        _modulate_kernel, grid=(B, S // ts), in_specs=[row, per_b, per_b], out_specs=row,
        out_shape=jax.ShapeDtypeStruct((B, S, D), bf16), compiler_params=_cparams(2), name="modulate",
    )(x, sc, sh)


def _mm_kernel(*refs, n_pairs):
    o_ref = refs[-1]
    acc = None
    for p in range(n_pairs):
        d = jnp.dot(refs[p][...], refs[n_pairs + p][...].astype(bf16), preferred_element_type=f32)
        acc = d if acc is None else acc + d
    o_ref[...] = acc.astype(o_ref.dtype)


def matmul(xs, ws, out_dtype, tm, tn, name, N=None, w_blocks=None):
    M = xs[0].shape[0]
    N = ws[0].shape[1] if N is None else N
    w_blocks = [(0, 0)] * len(ws) if w_blocks is None else w_blocks
    tm = min(tm, M)
    assert M % tm == 0 and N % tn == 0
    in_specs = [pl.BlockSpec((tm, x.shape[1]), lambda i, j: (i, 0)) for x in xs]
    in_specs += [pl.BlockSpec((x.shape[1], tn), lambda i, j, rb=rb, cb=cb: (rb, cb + j))
                 for x, (rb, cb) in zip(xs, w_blocks)]
    return pl.pallas_call(
        functools.partial(_mm_kernel, n_pairs=len(xs)),
        grid=(M // tm, N // tn), in_specs=in_specs,
        out_specs=pl.BlockSpec((tm, tn), lambda i, j: (i, j)),
        out_shape=jax.ShapeDtypeStruct((M, N), out_dtype), compiler_params=_cparams(2), name=name,
    )(*xs, *ws)


def _conv_kernel(cur_ref, halo_ref, w_ref, b_ref, o_ref, buf_ref):
    ts = cur_ref.shape[0]
    cur = cur_ref[...]
    buf_ref[0:SUBLANES, :] = jnp.where(pl.program_id(1) == 0, 0.0, halo_ref[...])
    buf_ref[SUBLANES:SUBLANES + ts, :] = cur
    acc = b_ref[...] + w_ref[CONV_K - 1:CONV_K, :] * cur
    for k in range(CONV_K - 1):
        acc += w_ref[k:k + 1, :] * buf_ref[pl.ds(SUBLANES - (CONV_K - 1 - k), ts), :]
    o_ref[...] = _silu(acc).astype(o_ref.dtype)


def conv_silu(src, col0, n_ch, conv_w, conv_b, ch0, out_dtype, ts=512, tc=512):
    B, S, _ = src.shape
    cb0, wb0 = col0 // tc, ch0 // tc
    hb = ts // SUBLANES
    return pl.pallas_call(
        _conv_kernel, grid=(B, S // ts, n_ch // tc),
        in_specs=[pl.BlockSpec((None, ts, tc), lambda b, i, j: (b, i, cb0 + j)),
                  pl.BlockSpec((None, SUBLANES, tc), lambda b, i, j: (b, jnp.maximum(i * hb - 1, 0), cb0 + j)),
                  pl.BlockSpec((CONV_K, tc), lambda b, i, j: (0, wb0 + j)),
                  pl.BlockSpec((1, tc), lambda b, i, j: (0, wb0 + j))],
        out_specs=pl.BlockSpec((None, ts, tc), lambda b, i, j: (b, i, j)),
        out_shape=jax.ShapeDtypeStruct((B, S, n_ch), out_dtype),
        scratch_shapes=[pltpu.VMEM((ts + SUBLANES, tc), f32)],
        compiler_params=_cparams(3), name="conv_silu",
    )(src, src, conv_w, conv_b.reshape(1, -1))


def _ssd_kernel(xs_ref, z_ref, dt_ref, bc_ref, a_ref, dtb_ref, dskip_ref, nw_ref, o_ref, state_ref, y_ref):
    L = SSM_CHUNK
    P = SSM_HEAD_DIM
    G = SSM_GROUPS
    GW = xs_ref.shape[1] // G
    NS = SSM_STATE

    @pl.when(pl.program_id(1) == 0)
    def _():
        state_ref[...] = jnp.zeros_like(state_ref)

    raw = dt_ref[...] + dtb_ref[...]
    dt = jnp.maximum(raw, 0.0) + jnp.log1p(jnp.exp(-jnp.abs(raw)))
    la = dt * a_ref[...]
    row = lax.broadcasted_iota(jnp.int32, (L, LANES), 0)
    cum = la
    k = 1
    while k < L:
        cum = cum + jnp.where(row >= k, pltpu.roll(cum, k, axis=0), 0.0)
        k *= 2
    cum_t = cum.T
    last = cum[L - 1:L, :]
    to_end = jnp.exp(last - cum)
    e_cum = jnp.exp(cum)
    c_dec = jnp.exp(last)
    causal = row >= lax.broadcasted_iota(jnp.int32, (L, L), 1)
    left = lax.broadcasted_iota(jnp.int32, (L, LANES), 1) < P
    left1 = left[0:1, :]

    for g in range(G):
        bm = bc_ref[:, g * NS:(g + 1) * NS]
        cm = bc_ref[:, G * NS + g * NS:G * NS + (g + 1) * NS]
        cb = lax.dot_general(cm, bm, (((1,), (1,)), ((), ())), preferred_element_type=f32)
        h_in = state_ref[g]
        y_off = jnp.dot(cm, h_in.astype(bf16), preferred_element_type=f32)
        xw_parts = []
        cd_parts = []
        for j in range(GW // LANES):
            h0 = (g * GW) // P + 2 * j
            c0 = g * GW + j * LANES

            def pair(m, h0=h0):
                return jnp.where(left, m[:, h0:h0 + 1], m[:, h0 + 1:h0 + 2])

            xs = xs_ref[:, c0:c0 + LANES]
            xdt = xs * pair(dt)
            xdt_b = xdt.astype(bf16)
            dec = []
            for h in (h0, h0 + 1):
                seg = cum[:, h:h + 1] - cum_t[h:h + 1, :]
                dec.append((cb * jnp.exp(jnp.where(causal, seg, -jnp.inf))).astype(bf16))
            m_cat = jnp.concatenate(dec, axis=1)
            zero = jnp.zeros_like(xdt_b)
            x_blk = jnp.concatenate([jnp.where(left, xdt_b, zero), jnp.where(left, zero, xdt_b)], axis=0)
            y_diag = jnp.dot(m_cat, x_blk, preferred_element_type=f32)
            y = y_diag + y_off[:, j * LANES:(j + 1) * LANES] * pair(e_cum) + dskip_ref[:, c0:c0 + LANES] * xs
            y_ref[:, c0:c0 + LANES] = y
            xw_parts.append((xdt * pair(to_end)).astype(bf16))
            cd_parts.append(jnp.where(left1, c_dec[:, h0:h0 + 1], c_dec[:, h0 + 1:h0 + 2]))
        xw = jnp.concatenate(xw_parts, axis=1)
        bm_t = bm.astype(f32).T.astype(bf16)
        st = jnp.dot(bm_t, xw, preferred_element_type=f32)
        state_ref[g] = h_in * jnp.concatenate(cd_parts, axis=1) + st

    for g in range(G):
        sl = slice(g * GW, (g + 1) * GW)
        gz = y_ref[:, sl] * _silu(z_ref[:, sl])
        ms = jnp.mean(gz * gz, axis=-1, keepdims=True)
        o_ref[:, sl] = (gz * lax.rsqrt(ms + LN_EPS) * nw_ref[:, sl]).astype(o_ref.dtype)


def ssd_mixer(xs, zx, dt_raw, bc, a_row, dtb_row, dskip_row, norm_w):
    B, S, DS = xs.shape
    L = SSM_CHUNK
    GW = DS // SSM_GROUPS
    blk = lambda w: pl.BlockSpec((None, L, w), lambda b, c: (b, c, 0))
    vec = lambda w: pl.BlockSpec((1, w), lambda b, c: (0, 0))
    return pl.pallas_call(
        _ssd_kernel, grid=(B, S // L),
        in_specs=[blk(DS), blk(DS), blk(LANES), blk(bc.shape[2]), vec(LANES), vec(LANES), vec(DS), vec(DS)],
        out_specs=blk(DS),
        out_shape=jax.ShapeDtypeStruct((B, S, DS), bf16),
        scratch_shapes=[pltpu.VMEM((SSM_GROUPS, SSM_STATE, GW), f32), pltpu.VMEM((L, DS), f32)],
        compiler_params=_cparams(2), name="ssd_mixer",
    )(xs, zx, dt_raw, bc, a_row, dtb_row, dskip_row, norm_w.reshape(1, DS))


def _pool_kernel(u_ref, halo_ref, w_ref, sc_ref, o_ref, *, halo_rows):
    g = pl.program_id(0)
    i = pl.program_id(2)
    ts = u_ref.shape[0]
    win = jnp.int32(POOL_WINDOWS[-1])
    for gi in range(len(POOL_WINDOWS) - 2, -1, -1):
        win = jnp.where(g == gi, jnp.int32(POOL_WINDOWS[gi]), win)
    u = u_ref[...]
    halo = jnp.where(i == 0, 0.0, halo_ref[...])
    s = jnp.concatenate([halo, u], axis=0)
    k = 1
    while k < POOL_WINDOWS[-1]:
        s = jnp.where(win > k, s + pltpu.roll(s, k, axis=0), s)
        k *= 2
    s = s[halo_rows:, :]
    t1 = i * ts + 1 + lax.broadcasted_iota(jnp.int32, (ts, 1), 0)
    cnt = jnp.minimum(t1, win).astype(f32)
    diff = (s / cnt - u).astype(bf16)
    o_ref[...] = (jnp.dot(diff, w_ref[...], preferred_element_type=f32) * sc_ref[...]).astype(o_ref.dtype)


def pool_mixer(u, pool_w_b, pool_scale, ts=512):
    B, S, DP = u.shape
    NG, GW, _ = pool_w_b.shape
    halo_rows = POOL_WINDOWS[-1]
    hb = ts // halo_rows
    return pl.pallas_call(
        functools.partial(_pool_kernel, halo_rows=halo_rows), grid=(NG, B, S // ts),
        in_specs=[pl.BlockSpec((None, ts, GW), lambda g, b, i: (b, i, g)),
                  pl.BlockSpec((None, halo_rows, GW), lambda g, b, i: (b, jnp.maximum(i * hb - 1, 0), g)),
                  pl.BlockSpec((None, GW, GW), lambda g, b, i: (g, 0, 0)),
                  pl.BlockSpec((1, GW), lambda g, b, i: (0, g))],
        out_specs=pl.BlockSpec((None, ts, GW), lambda g, b, i: (b, i, g)),
        out_shape=jax.ShapeDtypeStruct((B, S, DP), bf16),
        compiler_params=_cparams(3), name="pool_mixer",
    )(u, u, pool_w_b, pool_scale.reshape(1, DP))


def _ln_kernel(*refs, n_y, with_h, with_router):
    x_ref = refs[0]
    y_refs = refs[1:1 + n_y]
    g_ref, w_ref, b_ref = refs[1 + n_y:4 + n_y]
    pos = 4 + n_y
    y = y_refs[0][...].astype(f32)
    for r in y_refs[1:]:
        y = y + r[...].astype(f32)
    v = DN_ALPHA * x_ref[...] + (1.0 + g_ref[...]) * y
    mu = jnp.mean(v, axis=-1, keepdims=True)
    d = v - mu
    var = jnp.mean(d * d, axis=-1, keepdims=True)
    xn = d * lax.rsqrt(var + LN_EPS) * w_ref[...] + b_ref[...]
    if with_h:
        sc_ref, sh_ref = refs[pos:pos + 2]
        pos += 2
    if with_router:
        rw_ref, rb_ref = refs[pos:pos + 2]
        pos += 2
    xo_ref = refs[pos]
    xo_ref[...] = xn
    if with_h:
        h = xn * (1.0 + sc_ref[...]) + sh_ref[...]
        refs[pos + 1][...] = h.astype(bf16)
        if with_router:
            refs[pos + 2][...] = jnp.dot(h, rw_ref[...], preferred_element_type=f32,
                                         precision=lax.Precision.HIGHEST) + rb_ref[...]


def post_ln(x, ys, gate, ln_w, ln_b, sc=None, sh=None, router_w=None, router_b=None, ts=256):
    B, S, D = x.shape
    with_h = sc is not None
    with_router = router_w is not None
    row = pl.BlockSpec((None, ts, D), lambda b, i: (b, i, 0))
    per_b = pl.BlockSpec((None, 1, D), lambda b, i: (b, 0, 0))
    vec = pl.BlockSpec((1, D), lambda b, i: (0, 0))
    args = [x, *ys, gate, ln_w.reshape(1, D), ln_b.reshape(1, D)]
    in_specs = [row] * (1 + len(ys)) + [per_b, vec, vec]
    out_shape = [jax.ShapeDtypeStruct((B, S, D), f32)]
    out_specs = [row]
    if with_h:
        args += [sc, sh]
        in_specs += [per_b, per_b]
        out_shape.append(jax.ShapeDtypeStruct((B, S, D), bf16))
        out_specs.append(row)
    if with_router:
        NE = router_w.shape[1]
        rw = jnp.zeros((D, LANES), f32).at[:, :NE].set(router_w)
        rb = jnp.zeros((1, LANES), f32).at[0, :NE].set(router_b)
        args += [rw, rb]
        in_specs += [pl.BlockSpec((D, LANES), lambda b, i: (0, 0)), pl.BlockSpec((1, LANES), lambda b, i: (0, 0))]
        out_shape.append(jax.ShapeDtypeStruct((B, S, LANES), f32))
        out_specs.append(pl.BlockSpec((None, ts, LANES), lambda b, i: (b, i, 0)))
    return pl.pallas_call(
        functools.partial(_ln_kernel, n_y=len(ys), with_h=with_h, with_router=with_router),
        grid=(B, S // ts), in_specs=in_specs, out_specs=out_specs, out_shape=out_shape,
        compiler_params=_cparams(2), name="post_ln",
    )(*args)


def _cache_expert_weight(te_ref, w_ref, wbf_ref, rows_per_step):
    i = pl.program_id(0)
    changed = jnp.logical_or(i == 0, te_ref[i] != te_ref[jnp.maximum(i - 1, 0)])

    @pl.when(changed)
    def _():
        def body(r, carry):
            r0 = pl.multiple_of(r * rows_per_step, rows_per_step)
            wbf_ref[pl.ds(r0, rows_per_step), :] = w_ref[pl.ds(r0, rows_per_step), :].astype(bf16)
            return carry

        lax.fori_loop(0, w_ref.shape[0] // rows_per_step, body, 0)


def _expert_gu_kernel(te_ref, nu_ref, x_ref, wgu_ref, bgu_ref, sel_ref, h_ref, wbf_ref):
    i = pl.program_id(0)
    _cache_expert_weight(te_ref, wgu_ref, wbf_ref, 256)

    @pl.when(i < nu_ref[0])
    def _():
        gu = jnp.dot(x_ref[...], wbf_ref[...], preferred_element_type=f32) + bgu_ref[...]
        gate = jnp.minimum(gu, SWIGLU_LIMIT)
        up = jnp.clip(gu, -SWIGLU_LIMIT, SWIGLU_LIMIT)
        glu = gate * jax.nn.sigmoid(SWIGLU_ALPHA * gate)
        up_next = pltpu.roll(up, up.shape[1] - 1, axis=1)
        hf = ((up_next + 1.0) * glu).astype(bf16)
        h_ref[...] = jnp.dot(hf, sel_ref[...], preferred_element_type=f32).astype(bf16)

    @pl.when(i >= nu_ref[0])
    def _():
        h_ref[...] = jnp.zeros_like(h_ref)


def _expert_down_kernel(te_ref, nu_ref, h_ref, p_ref, wd_ref, bd_ref, o_ref, wbf_ref):
    i = pl.program_id(0)
    _cache_expert_weight(te_ref, wd_ref, wbf_ref, 64)

    @pl.when(i < nu_ref[0])
    def _():
        out = jnp.dot(h_ref[...], wbf_ref[...], preferred_element_type=f32) + bd_ref[...]
        o_ref[...] = (out * p_ref[...]).astype(o_ref.dtype)

    @pl.when(i >= nu_ref[0])
    def _():
        o_ref[...] = jnp.zeros_like(o_ref)


def expert_ffn(layer, tile_expert, n_used, xe, row_p, w_gu, bgu, sel, w_down, bd):
    R, D = xe.shape
    F2 = w_gu.shape[3]
    F = F2 // 2
    TM = MOE_TILE
    e_blk = lambda r, c: pl.BlockSpec((None, None, r, c), lambda i, te, nu: (layer, te[i], 0, 0))
    tile = lambda c: pl.BlockSpec((TM, c), lambda i, te, nu: (i, 0))
    h = pl.pallas_call(
        _expert_gu_kernel,
        grid_spec=pltpu.PrefetchScalarGridSpec(
            num_scalar_prefetch=2, grid=(R // TM,),
            in_specs=[tile(D), e_blk(D, F2), e_blk(1, F2), pl.BlockSpec((F2, F), lambda i, te, nu: (0, 0))],
            out_specs=tile(F), scratch_shapes=[pltpu.VMEM((D, F2), bf16)]),
        out_shape=jax.ShapeDtypeStruct((R, F), bf16), compiler_params=_cparams(1), name="expert_gate_up",
    )(tile_expert, n_used, xe, w_gu, bgu, sel)
    return pl.pallas_call(
        _expert_down_kernel,
        grid_spec=pltpu.PrefetchScalarGridSpec(
            num_scalar_prefetch=2, grid=(R // TM,),
            in_specs=[tile(F), tile(1), e_blk(F, D), e_blk(1, D)],
            out_specs=tile(D), scratch_shapes=[pltpu.VMEM((F, D), bf16)]),
        out_shape=jax.ShapeDtypeStruct((R, D), bf16), compiler_params=_cparams(1), name="expert_down",
    )(tile_expert, n_used, h, row_p, w_down, bd)


def moe_ffn(layer, h, logits, w_gu, b_gu, w_down, b_down):
    T, D = h.shape
    E, TM = N_EXPERTS, MOE_TILE
    F2 = w_gu.shape[3]
    top_v, top_e = lax.top_k(logits[:, :E], TOP_K)
    top_p = jax.nn.softmax(top_v, axis=-1)
    A = T * TOP_K
    flat_e = top_e.reshape(A)
    order = jnp.argsort(flat_e, stable=True).astype(jnp.int32)
    pos = jnp.argsort(order).astype(jnp.int32)
    seg_start = jnp.searchsorted(flat_e[order], jnp.arange(E, dtype=flat_e.dtype), side="left").astype(jnp.int32)
    counts = jnp.diff(jnp.concatenate([seg_start, jnp.array([A], jnp.int32)]))
    ntile_e = (counts + TM - 1) // TM
    tile_end = jnp.cumsum(ntile_e)
    pad_start = (tile_end - ntile_e) * TM
    dest = pad_start[flat_e] + pos - seg_start[flat_e]
    n_tiles = (A + E * (TM - 1)) // TM
    R = n_tiles * TM
    n_used = tile_end[-1:].astype(jnp.int32)
    tile_expert = jnp.minimum(jnp.searchsorted(tile_end, jnp.arange(n_tiles), side="right"), E - 1).astype(jnp.int32)
    tile_expert = jnp.where(jnp.arange(n_tiles) < n_used[0], tile_expert, tile_expert[n_used[0] - 1])
    tile_src = seg_start[tile_expert] + jnp.arange(n_tiles, dtype=jnp.int32) * TM - pad_start[tile_expert]
    tile_lim = (seg_start + counts)[tile_expert]
    src = (tile_src[:, None] + jnp.arange(TM, dtype=jnp.int32)[None, :]).reshape(R)
    valid = src < jnp.broadcast_to(tile_lim[:, None], (n_tiles, TM)).reshape(R)
    asg = order[jnp.clip(src, 0, A - 1)]
    row_tok = jnp.where(valid, asg // TOP_K, 0)
    row_p = jnp.where(valid, top_p.reshape(A)[asg], 0.0)
    xe = h[row_tok]
    sel = (jnp.arange(F2)[:, None] == 2 * jnp.arange(F2 // 2)[None, :]).astype(bf16)
    out = expert_ffn(layer, tile_expert, n_used, xe, row_p.reshape(R, 1), w_gu, b_gu[:, :, None, :], sel,
                     w_down, b_down[:, :, None, :])
    dest_k = dest.reshape(T, TOP_K)
    return [out[dest_k[:, k]] for k in range(TOP_K)]


def _rope(v, cos, sin):
    n = v.shape[-1]
    even = lax.broadcasted_iota(jnp.int32, v.shape, 1) % 2 == 0
    partner = jnp.where(even, pltpu.roll(v, n - 1, axis=1), pltpu.roll(v, 1, axis=1))
    return v * cos + partner * sin


def _rms(v, w, eps):
    return v * lax.rsqrt(jnp.mean(v * v, axis=-1, keepdims=True) + eps) * w


def _mla_a_kernel(x_ref, wq_ref, wkv_ref, qw_ref, kw_ref, cos_ref, sin_ref, qn_ref, kvc_ref, kpe_ref, *, kv_lora):
    x = x_ref[...]
    qa = jnp.dot(x, wq_ref[...], preferred_element_type=f32)
    qn_ref[...] = _rms(qa, qw_ref[...], RMS_EPS).astype(bf16)
    kv = jnp.dot(x, wkv_ref[...], preferred_element_type=f32)
    kvc_ref[...] = _rms(kv[:, :kv_lora], kw_ref[...], RMS_EPS).astype(bf16)
    kpe_ref[...] = _rope(kv[:, kv_lora:], cos_ref[...], sin_ref[...]).astype(bf16)


def mla_down(h, wq_a_b, wkv_a_b, q_norm_w, kv_norm_w, cos_k, sin_k, S, tm=512):
    T, D = h.shape
    QL = wq_a_b.shape[1]
    KVW = wkv_a_b.shape[1]
    KL = kv_norm_w.shape[0]
    nsb = S // tm
    full = lambda a: pl.BlockSpec(a.shape, lambda i: (0, 0))
    return pl.pallas_call(
        functools.partial(_mla_a_kernel, kv_lora=KL), grid=(T // tm,),
        in_specs=[pl.BlockSpec((tm, D), lambda i: (i, 0)), full(wq_a_b), full(wkv_a_b),
                  pl.BlockSpec((1, QL), lambda i: (0, 0)), pl.BlockSpec((1, KL), lambda i: (0, 0)),
                  pl.BlockSpec((tm, KVW - KL), lambda i: (i % nsb, 0)),
                  pl.BlockSpec((tm, KVW - KL), lambda i: (i % nsb, 0))],
        out_specs=[pl.BlockSpec((tm, QL), lambda i: (i, 0)), pl.BlockSpec((tm, KL), lambda i: (i, 0)),
                   pl.BlockSpec((tm, KVW - KL), lambda i: (i, 0))],
        out_shape=[jax.ShapeDtypeStruct((T, QL), bf16), jax.ShapeDtypeStruct((T, KL), bf16),
                   jax.ShapeDtypeStruct((T, KVW - KL), bf16)],
        compiler_params=_cparams(1), name="mla_down",
    )(h, wq_a_b, wkv_a_b, q_norm_w.reshape(1, QL), kv_norm_w.reshape(1, KL), cos_k, sin_k)


def _mla_attn_kernel(qn_ref, kvc_ref, kpe_ref, wq_ref, wkv_ref, cos_ref, sin_ref, o_ref, q_s, k_s, v_s, *, tq, c_exp):
    S = qn_ref.shape[0]
    nt = (((1,), (1,)), ((), ()))
    q = jnp.dot(qn_ref[...], wq_ref[...], preferred_element_type=f32)
    q_s[...] = _rope(q, cos_ref[...], sin_ref[...]).astype(bf16)
    kv = jnp.dot(kvc_ref[...], wkv_ref[...], preferred_element_type=f32)
    k_s[:, :QK_NOPE] = kv[:, :QK_NOPE].astype(bf16)
    k_s[:, QK_NOPE:] = kpe_ref[...]
    v_s[...] = kv[:, QK_NOPE:].astype(bf16)
    rows = lax.broadcasted_iota(jnp.int32, (tq, tq), 0)
    cols = lax.broadcasted_iota(jnp.int32, (tq, tq), 1)
    for i in range(S // tq):
        lo = i * tq
        qi = q_s[lo:lo + tq, :]
        s_d = lax.dot_general(qi, k_s[lo:lo + tq, :], nt, preferred_element_type=f32)
        s_d = jnp.where(rows >= cols, s_d, -jnp.inf)
        m = jnp.max(s_d, axis=-1, keepdims=True)
        if i > 0:
            s_o = lax.dot_general(qi, k_s[0:lo, :], nt, preferred_element_type=f32)
            m = jnp.maximum(m, jnp.max(s_o, axis=-1, keepdims=True))
        p_d = jnp.exp2((s_d - m) * c_exp)
        l = jnp.sum(p_d, axis=-1, keepdims=True)
        acc = jnp.dot(p_d.astype(bf16), v_s[lo:lo + tq, :], preferred_element_type=f32)
        if i > 0:
            p_o = jnp.exp2((s_o - m) * c_exp)
            l = l + jnp.sum(p_o, axis=-1, keepdims=True)
            acc = acc + jnp.dot(p_o.astype(bf16), v_s[0:lo, :], preferred_element_type=f32)
        o_ref[lo:lo + tq, :] = (acc / l).astype(o_ref.dtype)


def mla_attention(qn, kvc, kpe, wq_heads, wkv_heads, cos_q, sin_q, scale):
    B, S, QL = qn.shape
    KL = kvc.shape[2]
    H, _, W = wq_heads.shape
    tq = min(512, S)
    per_b = lambda w: pl.BlockSpec((None, S, w), lambda b, h: (b, 0, 0))
    per_h = lambda r, w: pl.BlockSpec((None, r, w), lambda b, h: (h, 0, 0))
    table = pl.BlockSpec((S, W), lambda b, h: (0, 0))
    return pl.pallas_call(
        functools.partial(_mla_attn_kernel, tq=tq, c_exp=scale * math.log2(math.e)), grid=(B, H),
        in_specs=[per_b(QL), per_b(KL), per_b(LANES), per_h(QL, W), per_h(KL, QK_NOPE + V_DIM), table, table],
        out_specs=pl.BlockSpec((None, S, V_DIM), lambda b, h: (b, 0, h)),
        out_shape=jax.ShapeDtypeStruct((B, S, H * V_DIM), bf16),
        scratch_shapes=[pltpu.VMEM((S, W), bf16), pltpu.VMEM((S, W), bf16), pltpu.VMEM((S, V_DIM), bf16)],
        compiler_params=_cparams(2), name="mla_attention",
    )(qn, kvc, kpe, wq_heads, wkv_heads, cos_q, sin_q)


def _rope_tables(S, width, rope0):
    half = QK_ROPE // 2
    inv_freq = ROPE_THETA ** (-jnp.arange(half, dtype=f32) * 2.0 / QK_ROPE)
    ang = jnp.arange(S).astype(f32)[:, None] * inv_freq[None, :]
    cos = jnp.repeat(jnp.cos(ang), 2, axis=1)
    sin = jnp.repeat(jnp.sin(ang), 2, axis=1) * jnp.tile(jnp.array([-1.0, 1.0], f32), half)[None, :]
    cos_t = jnp.ones((S, width), f32).at[:, rope0:rope0 + QK_ROPE].set(cos)
    sin_t = jnp.zeros((S, width), f32).at[:, rope0:rope0 + QK_ROPE].set(sin)
    return cos_t, sin_t


def mla_mixer(h, wq_a, q_norm_w, wq_b, wkv_a, kv_norm_w, wkv_b, wo, B, S):
    T, D = h.shape
    H = MLA_HEADS
    KL = kv_norm_w.shape[0]
    QL = wq_a.shape[1]
    W = 2 * LANES
    wkv_a_b = jnp.zeros((D, KL + LANES), f32).at[:, :KL + QK_ROPE].set(wkv_a).astype(bf16)
    wq_heads = jnp.zeros((H, QL, W), f32).at[:, :, :QK_NOPE + QK_ROPE].set(
        wq_b.reshape(QL, H, QK_NOPE + QK_ROPE).transpose(1, 0, 2)).astype(bf16)
    wkv_heads = wkv_b.reshape(KL, H, QK_NOPE + V_DIM).transpose(1, 0, 2).astype(bf16)
    cos_k, sin_k = _rope_tables(S, LANES, 0)
    cos_q, sin_q = _rope_tables(S, W, QK_NOPE)
    qn, kvc, kpe = mla_down(h, wq_a.astype(bf16), wkv_a_b, q_norm_w, kv_norm_w, cos_k, sin_k, S)
    o = mla_attention(qn.reshape(B, S, QL), kvc.reshape(B, S, KL), kpe.reshape(B, S, LANES), wq_heads, wkv_heads,
                      cos_q, sin_q, (QK_NOPE + QK_ROPE) ** -0.5)
    return matmul([o.reshape(T, H * V_DIM)], [wo], f32, 1024, 512, "attn_out_proj")


def ssd_pool_mixer(h, in_proj, conv_w, conv_b, dt_bias, a_log, d_skip, norm_w, pool_w, pool_scale, out_proj, B, S):
    T, D = h.shape
    DS = norm_w.shape[0]
    NH = dt_bias.shape[0]
    GN = SSM_GROUPS * SSM_STATE
    o1 = DS + DS + 2 * GN
    o2 = o1 + NH
    assert o1 % LANES == 0
    zx = matmul([h], [in_proj], f32, 1024, 512, "in_proj_zx", N=o1).reshape(B, S, o1)
    dt_raw = matmul([h], [in_proj], f32, 1024, LANES, "in_proj_dt", N=LANES,
                    w_blocks=[(0, o1 // LANES)]).reshape(B, S, LANES)
    u = matmul([h], [in_proj[:, o2:]], f32, 1024, 512, "in_proj_u").reshape(B, S, -1)
    xs = conv_silu(zx, DS, DS, conv_w, conv_b, 0, f32)
    bc = conv_silu(zx, 2 * DS, 2 * GN, conv_w, conv_b, DS, bf16)
    pad = lambda v: jnp.zeros((1, LANES), f32).at[0, :NH].set(v)
    a_row = pad(-jnp.exp(a_log))
    dskip_row = jnp.repeat(d_skip, SSM_HEAD_DIM).reshape(1, DS)
    y = ssd_mixer(xs, zx, dt_raw, bc, a_row, pad(dt_bias), dskip_row, norm_w)
    y_pool = pool_mixer(u, pool_w.astype(bf16), pool_scale)
    assert y_pool.shape[2] == DS
    return matmul([y.reshape(T, DS), y_pool.reshape(T, DS)], [out_proj, out_proj], f32, 1024, 256, "mixer_out_proj",
                  w_blocks=[(0, 0), (1, 0)])


def kernel(x, c, ada_w, ada_b, ln_w, ln_b, in_proj, conv_w, conv_b, dt_bias, a_log, d_skip, ssd_norm_w, pool_w, pool_scale, out_proj, wq_a, q_norm_w, wq_b, wkv_a, kv_norm_w, wkv_b, wo, router_w, router_b, w_gu, b_gu, w_down, b_down):
    B, S, D = x.shape
    T = B * S
    mod = ada_modulation(c, ada_w, ada_b)
    m = lambda l, k: mod[l, :, k].reshape(B, 1, D)
    h = modulate(x, m(0, 1), m(0, 0))
    for i in range(DEPTH):
        j = i // 2
        hf = h.reshape(T, D)
        if i % 2 == 0:
            y = ssd_pool_mixer(hf, in_proj[j], conv_w[j], conv_b[j], dt_bias[j], a_log[j], d_skip[j],
                               ssd_norm_w[j], pool_w[j], pool_scale[j], out_proj[j], B, S)
        else:
            y = mla_mixer(hf, wq_a[j], q_norm_w[j], wq_b[j], wkv_a[j], kv_norm_w[j], wkv_b[j], wo[j], B, S)
        x, hff, logits = post_ln(x, [y.reshape(B, S, D)], m(i, 2), ln_w[i, 0], ln_b[i, 0], m(i, 4), m(i, 3),
                                 router_w[i], router_b[i])
        ys = moe_ffn(i, hff.reshape(T, D), logits.reshape(T, LANES), w_gu, b_gu, w_down, b_down)
        ys = [y.reshape(B, S, D) for y in ys]
        if i + 1 < DEPTH:
            x, h = post_ln(x, ys, m(i, 5), ln_w[i, 1], ln_b[i, 1], m(i + 1, 1), m(i + 1, 0))
        else:
            (x,) = post_ln(x, ys, m(i, 5), ln_w[i, 1], ln_b[i, 1])
    return x
```

```python
import functools
import math

import jax
import jax.numpy as jnp
from jax import lax
from jax.experimental import pallas as pl
from jax.experimental.pallas import tpu as pltpu

f32 = jnp.float32
bf16 = jnp.bfloat16

DEPTH = 2
DN_ALPHA = (2 * DEPTH) ** 0.25
LN_EPS = 1e-5
RMS_EPS = 1e-6
SSM_HEAD_DIM = 64
SSM_GROUPS = 8
SSM_STATE = 128
SSM_CHUNK = 128
CONV_K = 4
POOL_WINDOWS = (2, 4, 8, 16)
MLA_HEADS = 32
QK_NOPE = 128
QK_ROPE = 64
V_DIM = 128
ROPE_THETA = 10000.0
N_EXPERTS = 32
TOP_K = 4
SWIGLU_LIMIT = 7.0
SWIGLU_ALPHA = 1.702

LANES = 128
SUBLANES = 8
VMEM_LIMIT_BYTES = 56 * 1024 * 1024

MOE_TILE = 256


def _cparams(n_axes):
    return pltpu.CompilerParams(dimension_semantics=("arbitrary",) * n_axes, vmem_limit_bytes=VMEM_LIMIT_BYTES)


def _silu(v):
    return v * jax.nn.sigmoid(v)


def _ada_kernel(c_ref, w_ref, b_ref, o_ref):
    cond = _silu(c_ref[...])
    o_ref[...] = jnp.dot(cond.astype(bf16), w_ref[...].astype(bf16), preferred_element_type=f32) + b_ref[...]


def ada_modulation(c, ada_w, ada_b):
    B, D = c.shape
    L, _, N = ada_w.shape
    tn = 512
    cp = jnp.zeros((SUBLANES, D), f32).at[:B].set(c)
    out = pl.pallas_call(
        _ada_kernel,
        grid=(L, N // tn),
        in_specs=[pl.BlockSpec((SUBLANES, D), lambda l, j: (0, 0)),
                  pl.BlockSpec((None, D, tn), lambda l, j: (l, 0, j)),
                  pl.BlockSpec((None, 1, tn), lambda l, j: (l, 0, j))],
        out_specs=pl.BlockSpec((None, SUBLANES, tn), lambda l, j: (l, 0, j)),
        out_shape=jax.ShapeDtypeStruct((L, SUBLANES, N), f32),
        compiler_params=_cparams(2), name="ada_modulation",
    )(cp, ada_w, ada_b.reshape(L, 1, N))
    return out[:, :B].reshape(L, B, 6, D)


def _modulate_kernel(x_ref, sc_ref, sh_ref, o_ref):
    o_ref[...] = (x_ref[...] * (1.0 + sc_ref[...]) + sh_ref[...]).astype(o_ref.dtype)


def modulate(x, sc, sh, ts=512):
    B, S, D = x.shape
    row = pl.BlockSpec((None, ts, D), lambda b, i: (b, i, 0))
    per_b = pl.BlockSpec((None, 1, D), lambda b, i: (b, 0, 0))
    return pl.pallas_call(
        _modulate_kernel, grid=(B, S // ts), in_specs=[row, per_b, per_b], out_specs=row,
        out_shape=jax.ShapeDtypeStruct((B, S, D), bf16), compiler_params=_cparams(2), name="modulate",
    )(x, sc, sh)


def _mm_kernel(*refs, n_pairs):
    o_ref = refs[-1]
    acc = None
    for p in range(n_pairs):
        d = jnp.dot(refs[p][...], refs[n_pairs + p][...].astype(bf16), preferred_element_type=f32)
        acc = d if acc is None else acc + d
    o_ref[...] = acc.astype(o_ref.dtype)


def matmul(xs, ws, out_dtype, tm, tn, name, N=None, w_blocks=None):
    M = xs[0].shape[0]
    N = ws[0].shape[1] if N is None else N
    w_blocks = [(0, 0)] * len(ws) if w_blocks is None else w_blocks
    tm = min(tm, M)
    assert M % tm == 0 and N % tn == 0
    in_specs = [pl.BlockSpec((tm, x.shape[1]), lambda i, j: (i, 0)) for x in xs]
    in_specs += [pl.BlockSpec((x.shape[1], tn), lambda i, j, rb=rb, cb=cb: (rb, cb + j))
                 for x, (rb, cb) in zip(xs, w_blocks)]
    return pl.pallas_call(
        functools.partial(_mm_kernel, n_pairs=len(xs)),
        grid=(M // tm, N // tn), in_specs=in_specs,
        out_specs=pl.BlockSpec((tm, tn), lambda i, j: (i, j)),
        out_shape=jax.ShapeDtypeStruct((M, N), out_dtype), compiler_params=_cparams(2), name=name,
    )(*xs, *ws)


def _conv_kernel(cur_ref, halo_ref, w_ref, b_ref, o_ref, buf_ref):
    ts = cur_ref.shape[0]
    cur = cur_ref[...]
    buf_ref[0:SUBLANES, :] = jnp.where(pl.program_id(1) == 0, 0.0, halo_ref[...])
    buf_ref[SUBLANES:SUBLANES + ts, :] = cur
    acc = b_ref[...] + w_ref[CONV_K - 1:CONV_K, :] * cur
    for k in range(CONV_K - 1):
        acc += w_ref[k:k + 1, :] * buf_ref[pl.ds(SUBLANES - (CONV_K - 1 - k), ts), :]
    o_ref[...] = _silu(acc).astype(o_ref.dtype)


def conv_silu(src, col0, n_ch, conv_w, conv_b, ch0, out_dtype, ts=512, tc=512):
    B, S, _ = src.shape
    cb0, wb0 = col0 // tc, ch0 // tc
    hb = ts // SUBLANES
    return pl.pallas_call(
        _conv_kernel, grid=(B, S // ts, n_ch // tc),
        in_specs=[pl.BlockSpec((None, ts, tc), lambda b, i, j: (b, i, cb0 + j)),
                  pl.BlockSpec((None, SUBLANES, tc), lambda b, i, j: (b, jnp.maximum(i * hb - 1, 0), cb0 + j)),
                  pl.BlockSpec((CONV_K, tc), lambda b, i, j: (0, wb0 + j)),
                  pl.BlockSpec((1, tc), lambda b, i, j: (0, wb0 + j))],
        out_specs=pl.BlockSpec((None, ts, tc), lambda b, i, j: (b, i, j)),
        out_shape=jax.ShapeDtypeStruct((B, S, n_ch), out_dtype),
        scratch_shapes=[pltpu.VMEM((ts + SUBLANES, tc), f32)],
        compiler_params=_cparams(3), name="conv_silu",
    )(src, src, conv_w, conv_b.reshape(1, -1))


def _ssd_kernel(xs_ref, z_ref, dt_ref, bc_ref, a_ref, dtb_ref, dskip_ref, nw_ref, o_ref, state_ref, y_ref):
    L = SSM_CHUNK
    P = SSM_HEAD_DIM
    G = SSM_GROUPS
    GW = xs_ref.shape[1] // G
    NS = SSM_STATE

    @pl.when(pl.program_id(1) == 0)
    def _():
        state_ref[...] = jnp.zeros_like(state_ref)

    raw = dt_ref[...] + dtb_ref[...]
    dt = jnp.maximum(raw, 0.0) + jnp.log1p(jnp.exp(-jnp.abs(raw)))
    la = dt * a_ref[...]
    row = lax.broadcasted_iota(jnp.int32, (L, LANES), 0)
    cum = la
    k = 1
    while k < L:
        cum = cum + jnp.where(row >= k, pltpu.roll(cum, k, axis=0), 0.0)
        k *= 2
    cum_t = cum.T
    last = cum[L - 1:L, :]
    to_end = jnp.exp(last - cum)
    e_cum = jnp.exp(cum)
    c_dec = jnp.exp(last)
    causal = row >= lax.broadcasted_iota(jnp.int32, (L, L), 1)
    left = lax.broadcasted_iota(jnp.int32, (L, LANES), 1) < P
    left1 = left[0:1, :]

    for g in range(G):
        bm = bc_ref[:, g * NS:(g + 1) * NS]
        cm = bc_ref[:, G * NS + g * NS:G * NS + (g + 1) * NS]
        cb = lax.dot_general(cm, bm, (((1,), (1,)), ((), ())), preferred_element_type=f32)
        h_in = state_ref[g]
        y_off = jnp.dot(cm, h_in.astype(bf16), preferred_element_type=f32)
        xw_parts = []
        cd_parts = []
        for j in range(GW // LANES):
            h0 = (g * GW) // P + 2 * j
            c0 = g * GW + j * LANES

            def pair(m, h0=h0):
                return jnp.where(left, m[:, h0:h0 + 1], m[:, h0 + 1:h0 + 2])

            xs = xs_ref[:, c0:c0 + LANES]
            xdt = xs * pair(dt)
            xdt_b = xdt.astype(bf16)
            dec = []
            for h in (h0, h0 + 1):
                seg = cum[:, h:h + 1] - cum_t[h:h + 1, :]
                dec.append((cb * jnp.exp(jnp.where(causal, seg, -jnp.inf))).astype(bf16))
            m_cat = jnp.concatenate(dec, axis=1)
            zero = jnp.zeros_like(xdt_b)
            x_blk = jnp.concatenate([jnp.where(left, xdt_b, zero), jnp.where(left, zero, xdt_b)], axis=0)
            y_diag = jnp.dot(m_cat, x_blk, preferred_element_type=f32)
            y = y_diag + y_off[:, j * LANES:(j + 1) * LANES] * pair(e_cum) + dskip_ref[:, c0:c0 + LANES] * xs
            y_ref[:, c0:c0 + LANES] = y
            xw_parts.append((xdt * pair(to_end)).astype(bf16))
            cd_parts.append(jnp.where(left1, c_dec[:, h0:h0 + 1], c_dec[:, h0 + 1:h0 + 2]))
        xw = jnp.concatenate(xw_parts, axis=1)
        bm_t = bm.astype(f32).T.astype(bf16)
        st = jnp.dot(bm_t, xw, preferred_element_type=f32)
        state_ref[g] = h_in * jnp.concatenate(cd_parts, axis=1) + st

    for g in range(G):
        sl = slice(g * GW, (g + 1) * GW)
        gz = y_ref[:, sl] * _silu(z_ref[:, sl])
        ms = jnp.mean(gz * gz, axis=-1, keepdims=True)
        o_ref[:, sl] = (gz * lax.rsqrt(ms + LN_EPS) * nw_ref[:, sl]).astype(o_ref.dtype)


def ssd_mixer(xs, zx, dt_raw, bc, a_row, dtb_row, dskip_row, norm_w):
    B, S, DS = xs.shape
    L = SSM_CHUNK
    GW = DS // SSM_GROUPS
    blk = lambda w: pl.BlockSpec((None, L, w), lambda b, c: (b, c, 0))
    vec = lambda w: pl.BlockSpec((1, w), lambda b, c: (0, 0))
    return pl.pallas_call(
        _ssd_kernel, grid=(B, S // L),
        in_specs=[blk(DS), blk(DS), blk(LANES), blk(bc.shape[2]), vec(LANES), vec(LANES), vec(DS), vec(DS)],
        out_specs=blk(DS),
        out_shape=jax.ShapeDtypeStruct((B, S, DS), bf16),
        scratch_shapes=[pltpu.VMEM((SSM_GROUPS, SSM_STATE, GW), f32), pltpu.VMEM((L, DS), f32)],
        compiler_params=_cparams(2), name="ssd_mixer",
    )(xs, zx, dt_raw, bc, a_row, dtb_row, dskip_row, norm_w.reshape(1, DS))


def _pool_kernel(u_ref, halo_ref, w_ref, sc_ref, o_ref, *, halo_rows):
    g = pl.program_id(0)
    i = pl.program_id(2)
    ts = u_ref.shape[0]
    win = jnp.int32(POOL_WINDOWS[-1])
    for gi in range(len(POOL_WINDOWS) - 2, -1, -1):
        win = jnp.where(g == gi, jnp.int32(POOL_WINDOWS[gi]), win)
    u = u_ref[...]
    halo = jnp.where(i == 0, 0.0, halo_ref[...])
    s = jnp.concatenate([halo, u], axis=0)
    k = 1
    while k < POOL_WINDOWS[-1]:
        s = jnp.where(win > k, s + pltpu.roll(s, k, axis=0), s)
        k *= 2
    s = s[halo_rows:, :]
    t1 = i * ts + 1 + lax.broadcasted_iota(jnp.int32, (ts, 1), 0)
    cnt = jnp.minimum(t1, win).astype(f32)
    diff = (s / cnt - u).astype(bf16)
    o_ref[...] = (jnp.dot(diff, w_ref[...], preferred_element_type=f32) * sc_ref[...]).astype(o_ref.dtype)


def pool_mixer(u, pool_w_b, pool_scale, ts=512):
    B, S, DP = u.shape
    NG, GW, _ = pool_w_b.shape
    halo_rows = POOL_WINDOWS[-1]
    hb = ts // halo_rows
    return pl.pallas_call(
        functools.partial(_pool_kernel, halo_rows=halo_rows), grid=(NG, B, S // ts),
        in_specs=[pl.BlockSpec((None, ts, GW), lambda g, b, i: (b, i, g)),
                  pl.BlockSpec((None, halo_rows, GW), lambda g, b, i: (b, jnp.maximum(i * hb - 1, 0), g)),
                  pl.BlockSpec((None, GW, GW), lambda g, b, i: (g, 0, 0)),
                  pl.BlockSpec((1, GW), lambda g, b, i: (0, g))],
        out_specs=pl.BlockSpec((None, ts, GW), lambda g, b, i: (b, i, g)),
        out_shape=jax.ShapeDtypeStruct((B, S, DP), bf16),
        compiler_params=_cparams(3), name="pool_mixer",
    )(u, u, pool_w_b, pool_scale.reshape(1, DP))


def _pack_halves(v):
    bits = lax.bitcast_convert_type(v.astype(bf16).astype(f32), jnp.uint32)
    half = v.shape[1] // 2
    return (bits[:, :half] >> 16) | (bits[:, half:] & jnp.uint32(0xFFFF0000))


def _unpack_halves(u):
    lo = lax.bitcast_convert_type(u << 16, f32)
    hi = lax.bitcast_convert_type(u & jnp.uint32(0xFFFF0000), f32)
    return lo, hi


def _ln_kernel(*refs, n_y, packed_y, h_mode, with_router):
    x_ref = refs[0]
    y_refs = refs[1:1 + n_y]
    g_ref, w_ref, b_ref = refs[1 + n_y:4 + n_y]
    pos = 4 + n_y
    if packed_y:
        lo, hi = _unpack_halves(y_refs[0][...])
        for r in y_refs[1:]:
            lo_r, hi_r = _unpack_halves(r[...])
            lo, hi = lo + lo_r, hi + hi_r
        y = jnp.concatenate([lo, hi], axis=1)
    else:
        y = y_refs[0][...].astype(f32)
        for r in y_refs[1:]:
            y = y + r[...].astype(f32)
    v = DN_ALPHA * x_ref[...] + (1.0 + g_ref[...]) * y
    mu = jnp.mean(v, axis=-1, keepdims=True)
    d = v - mu
    var = jnp.mean(d * d, axis=-1, keepdims=True)
    xn = d * lax.rsqrt(var + LN_EPS) * w_ref[...] + b_ref[...]
    if h_mode is not None:
        sc_ref, sh_ref = refs[pos:pos + 2]
        pos += 2
    if with_router:
        rw_ref, rb_ref = refs[pos:pos + 2]
        pos += 2
    xo_ref = refs[pos]
    xo_ref[...] = xn
    if h_mode is not None:
        h = xn * (1.0 + sc_ref[...]) + sh_ref[...]
        refs[pos + 1][...] = _pack_halves(h) if h_mode == "packed" else h.astype(bf16)
        if with_router:
            refs[pos + 2][...] = jnp.dot(h, rw_ref[...], preferred_element_type=f32,
                                         precision=lax.Precision.HIGHEST) + rb_ref[...]


def post_ln(x, ys, gate, ln_w, ln_b, sc=None, sh=None, router_w=None, router_b=None, packed_y=False, ts=256):
    B, S, D = x.shape
    h_mode = None if sc is None else ("packed" if router_w is not None else "bf16")
    with_router = router_w is not None
    nsb = S // ts
    row = pl.BlockSpec((None, ts, D), lambda b, i: (b, i, 0))
    per_b = pl.BlockSpec((None, 1, D), lambda b, i: (b, 0, 0))
    vec = pl.BlockSpec((1, D), lambda b, i: (0, 0))
    if packed_y:
        y_args = [ys[0]] * TOP_K
        y_specs = [pl.BlockSpec((ts, D // 2), lambda b, i, k=k: (k * B * nsb + b * nsb + i, 0)) for k in range(TOP_K)]
    else:
        y_args = list(ys)
        y_specs = [row] * len(ys)
    args = [x, *y_args, gate, ln_w.reshape(1, D), ln_b.reshape(1, D)]
    in_specs = [row] + y_specs + [per_b, vec, vec]
    out_shape = [jax.ShapeDtypeStruct((B, S, D), f32)]
    out_specs = [row]
    if h_mode == "bf16":
        args += [sc, sh]
        in_specs += [per_b, per_b]
        out_shape.append(jax.ShapeDtypeStruct((B, S, D), bf16))
        out_specs.append(row)
    elif h_mode == "packed":
        args += [sc, sh]
        in_specs += [per_b, per_b]
        out_shape.append(jax.ShapeDtypeStruct((B, S, D // 2), jnp.uint32))
        out_specs.append(pl.BlockSpec((None, ts, D // 2), lambda b, i: (b, i, 0)))
    if with_router:
        NE = router_w.shape[1]
        rw = jnp.zeros((D, LANES), f32).at[:, :NE].set(router_w)
        rb = jnp.zeros((1, LANES), f32).at[0, :NE].set(router_b)
        args += [rw, rb]
        in_specs += [pl.BlockSpec((D, LANES), lambda b, i: (0, 0)), pl.BlockSpec((1, LANES), lambda b, i: (0, 0))]
        out_shape.append(jax.ShapeDtypeStruct((B, S, LANES), f32))
        out_specs.append(pl.BlockSpec((None, ts, LANES), lambda b, i: (b, i, 0)))
    return pl.pallas_call(
        functools.partial(_ln_kernel, n_y=len(y_args), packed_y=packed_y, h_mode=h_mode, with_router=with_router),
        grid=(B, nsb), in_specs=in_specs, out_specs=out_specs, out_shape=out_shape,
        compiler_params=_cparams(2), name="post_ln",
    )(*args)


def _cache_expert_weight(te_ref, w_ref, wbf_ref, rows_per_step):
    i = pl.program_id(0)
    changed = jnp.logical_or(i == 0, te_ref[i] != te_ref[jnp.maximum(i - 1, 0)])

    @pl.when(changed)
    def _():
        def body(r, carry):
            r0 = pl.multiple_of(r * rows_per_step, rows_per_step)
            wbf_ref[pl.ds(r0, rows_per_step), :] = w_ref[pl.ds(r0, rows_per_step), :].astype(bf16)
            return carry

        lax.fori_loop(0, w_ref.shape[0] // rows_per_step, body, 0)


DMA_ISSUE_UNROLL = 8


def _tile_rows_copy(hbm_ref, buf_ref, sem_ref, slot):
    return pltpu.make_async_copy(hbm_ref.at[pl.ds(0, buf_ref.shape[1]), :], buf_ref.at[slot], sem_ref.at[slot])


def _expert_gu_kernel(te_ref, nu_ref, tok_ref, tok_next_ref, hp_hbm, wgu_ref, bgu_ref, sel_ref, h_ref,
                      wbf_ref, xbuf, sem, *, n_steps):
    i = pl.program_id(0)
    slot = i % 2
    n_used = nu_ref[0]
    TM = xbuf.shape[1]
    half = xbuf.shape[2]

    def gather_rows(idx_ref, s):
        def body(r, carry):
            pltpu.make_async_copy(hp_hbm.at[pl.ds(idx_ref[0, r], 1), :], xbuf.at[s, pl.ds(r, 1), :], sem.at[s]).start()
            return carry

        lax.fori_loop(0, TM, body, 0, unroll=DMA_ISSUE_UNROLL)

    @pl.when(jnp.logical_and(i == 0, n_used > 0))
    def _():
        gather_rows(tok_ref, 0)

    @pl.when(jnp.logical_and(i + 1 < n_steps, i + 1 < n_used))
    def _():
        gather_rows(tok_next_ref, 1 - slot)

    _cache_expert_weight(te_ref, wgu_ref, wbf_ref, 256)

    @pl.when(i < n_used)
    def _():
        _tile_rows_copy(hp_hbm, xbuf, sem, slot).wait()
        x_lo, x_hi = _unpack_halves(xbuf[slot])
        gu = (jnp.dot(x_lo.astype(bf16), wbf_ref[0:half, :], preferred_element_type=f32)
              + jnp.dot(x_hi.astype(bf16), wbf_ref[half:2 * half, :], preferred_element_type=f32) + bgu_ref[...])
        gate = jnp.minimum(gu, SWIGLU_LIMIT)
        up = jnp.clip(gu, -SWIGLU_LIMIT, SWIGLU_LIMIT)
        glu = gate * jax.nn.sigmoid(SWIGLU_ALPHA * gate)
        up_next = pltpu.roll(up, up.shape[1] - 1, axis=1)
        hf = ((up_next + 1.0) * glu).astype(bf16)
        h_ref[...] = jnp.dot(hf, sel_ref[...], preferred_element_type=f32).astype(bf16)

    @pl.when(i >= n_used)
    def _():
        h_ref[...] = jnp.zeros_like(h_ref)


def _expert_down_kernel(te_ref, nu_ref, dst_ref, h_ref, p_ref, wd_ref, bd_ref, yp_hbm, wbf_ref, obuf, sem, *, n_steps):
    i = pl.program_id(0)
    slot = i % 2
    TM = obuf.shape[1]
    _cache_expert_weight(te_ref, wd_ref, wbf_ref, 64)

    @pl.when(i >= 2)
    def _():
        _tile_rows_copy(yp_hbm, obuf, sem, slot).wait()

    @pl.when(i < nu_ref[0])
    def _():
        out = jnp.dot(h_ref[...], wbf_ref[...], preferred_element_type=f32) + bd_ref[...]
        obuf[slot] = _pack_halves(out * p_ref[...])

    @pl.when(i >= nu_ref[0])
    def _():
        obuf[slot] = jnp.zeros(obuf.shape[1:], obuf.dtype)

    def body(r, carry):
        pltpu.make_async_copy(obuf.at[slot, pl.ds(r, 1), :], yp_hbm.at[pl.ds(dst_ref[0, r], 1), :], sem.at[slot]).start()
        return carry

    lax.fori_loop(0, TM, body, 0, unroll=DMA_ISSUE_UNROLL)

    @pl.when(i == n_steps - 1)
    def _():
        _tile_rows_copy(yp_hbm, obuf, sem, slot).wait()
        if n_steps > 1:
            _tile_rows_copy(yp_hbm, obuf, sem, 1 - slot).wait()


def expert_ffn(layer, tile_expert, n_used, hp, row_tok, row_dst, row_p, w_gu, bgu, sel, w_down, bd):
    n_tiles, _, TM = row_tok.shape
    R = n_tiles * TM
    half = hp.shape[1]
    D = 2 * half
    F2 = w_gu.shape[3]
    F = F2 // 2
    e_blk = lambda r, c: pl.BlockSpec((None, None, r, c), lambda i, te, nu: (layer, te[i], 0, 0))
    tile = lambda c: pl.BlockSpec((TM, c), lambda i, te, nu: (i, 0))
    idx = lambda off: pl.BlockSpec((None, 1, TM), lambda i, te, nu: (jnp.minimum(i + off, n_tiles - 1), 0, 0),
                                   memory_space=pltpu.SMEM)
    hbm = pl.BlockSpec(memory_space=pl.ANY)
    h = pl.pallas_call(
        functools.partial(_expert_gu_kernel, n_steps=n_tiles),
        grid_spec=pltpu.PrefetchScalarGridSpec(
            num_scalar_prefetch=2, grid=(n_tiles,),
            in_specs=[idx(0), idx(1), hbm, e_blk(D, F2), e_blk(1, F2), pl.BlockSpec((F2, F), lambda i, te, nu: (0, 0))],
            out_specs=tile(F),
            scratch_shapes=[pltpu.VMEM((D, F2), bf16), pltpu.VMEM((2, TM, half), jnp.uint32),
                            pltpu.SemaphoreType.DMA((2,))]),
        out_shape=jax.ShapeDtypeStruct((R, F), bf16), compiler_params=_cparams(1), name="expert_gate_up",
    )(tile_expert, n_used, row_tok, row_tok, hp, w_gu, bgu, sel)
    return pl.pallas_call(
        functools.partial(_expert_down_kernel, n_steps=n_tiles),
        grid_spec=pltpu.PrefetchScalarGridSpec(
            num_scalar_prefetch=2, grid=(n_tiles,),
            in_specs=[idx(0), tile(F), tile(1), e_blk(F, D), e_blk(1, D)],
            out_specs=hbm,
            scratch_shapes=[pltpu.VMEM((F, D), bf16), pltpu.VMEM((2, TM, half), jnp.uint32),
                            pltpu.SemaphoreType.DMA((2,))]),
        out_shape=jax.ShapeDtypeStruct((R, half), jnp.uint32), compiler_params=_cparams(1), name="expert_down",
    )(tile_expert, n_used, row_dst, h, row_p, w_down, bd)


def moe_ffn(layer, hp, logits, w_gu, b_gu, w_down, b_down):
    T = hp.shape[0]
    E, TM = N_EXPERTS, MOE_TILE
    F2 = w_gu.shape[3]
    top_v, top_e = lax.top_k(logits[:, :E], TOP_K)
    top_p = jax.nn.softmax(top_v, axis=-1)
    A = T * TOP_K
    flat_e = top_e.reshape(A)
    order = jnp.argsort(flat_e, stable=True).astype(jnp.int32)
    seg_start = jnp.searchsorted(flat_e[order], jnp.arange(E, dtype=flat_e.dtype), side="left").astype(jnp.int32)
    counts = jnp.diff(jnp.concatenate([seg_start, jnp.array([A], jnp.int32)]))
    ntile_e = (counts + TM - 1) // TM
    tile_end = jnp.cumsum(ntile_e)
    pad_start = (tile_end - ntile_e) * TM
    n_tiles = (A + E * (TM - 1)) // TM
    R = n_tiles * TM
    n_used = tile_end[-1:].astype(jnp.int32)
    tile_expert = jnp.minimum(jnp.searchsorted(tile_end, jnp.arange(n_tiles), side="right"), E - 1).astype(jnp.int32)
    tile_expert = jnp.where(jnp.arange(n_tiles) < n_used[0], tile_expert, tile_expert[n_used[0] - 1])
    tile_src = seg_start[tile_expert] + jnp.arange(n_tiles, dtype=jnp.int32) * TM - pad_start[tile_expert]
    tile_lim = (seg_start + counts)[tile_expert]
    src = (tile_src[:, None] + jnp.arange(TM, dtype=jnp.int32)[None, :]).reshape(R)
    valid = src < jnp.broadcast_to(tile_lim[:, None], (n_tiles, TM)).reshape(R)
    asg = order[jnp.clip(src, 0, A - 1)]
    row_tok = jnp.where(valid, asg // TOP_K, 0)
    row_p = jnp.where(valid, top_p.reshape(A)[asg], 0.0)
    pad_rank = jnp.cumsum(jnp.logical_not(valid).astype(jnp.int32)) - 1
    row_dst = jnp.where(valid, (asg % TOP_K) * T + asg // TOP_K, A + pad_rank)
    sel = (jnp.arange(F2)[:, None] == 2 * jnp.arange(F2 // 2)[None, :]).astype(bf16)
    return expert_ffn(layer, tile_expert, n_used, hp, row_tok.reshape(n_tiles, 1, TM), row_dst.reshape(n_tiles, 1, TM),
                      row_p.reshape(R, 1), w_gu, b_gu[:, :, None, :], sel, w_down, b_down[:, :, None, :])


def _rope(v, cos, sin):
    n = v.shape[-1]
    even = lax.broadcasted_iota(jnp.int32, v.shape, 1) % 2 == 0
    partner = jnp.where(even, pltpu.roll(v, n - 1, axis=1), pltpu.roll(v, 1, axis=1))
    return v * cos + partner * sin


def _rms(v, w, eps):
    return v * lax.rsqrt(jnp.mean(v * v, axis=-1, keepdims=True) + eps) * w


def _mla_a_kernel(x_ref, wq_ref, wkv_ref, qw_ref, kw_ref, cos_ref, sin_ref, qn_ref, kvc_ref, kpe_ref, *, kv_lora):
    x = x_ref[...]
    qa = jnp.dot(x, wq_ref[...], preferred_element_type=f32)
    qn_ref[...] = _rms(qa, qw_ref[...], RMS_EPS).astype(bf16)
    kv = jnp.dot(x, wkv_ref[...], preferred_element_type=f32)
    kvc_ref[...] = _rms(kv[:, :kv_lora], kw_ref[...], RMS_EPS).astype(bf16)
    kpe_ref[...] = _rope(kv[:, kv_lora:], cos_ref[...], sin_ref[...]).astype(bf16)


def mla_down(h, wq_a_b, wkv_a_b, q_norm_w, kv_norm_w, cos_k, sin_k, S, tm=512):
    T, D = h.shape
    QL = wq_a_b.shape[1]
    KVW = wkv_a_b.shape[1]
    KL = kv_norm_w.shape[0]
    nsb = S // tm
    full = lambda a: pl.BlockSpec(a.shape, lambda i: (0, 0))
    return pl.pallas_call(
        functools.partial(_mla_a_kernel, kv_lora=KL), grid=(T // tm,),
        in_specs=[pl.BlockSpec((tm, D), lambda i: (i, 0)), full(wq_a_b), full(wkv_a_b),
                  pl.BlockSpec((1, QL), lambda i: (0, 0)), pl.BlockSpec((1, KL), lambda i: (0, 0)),
                  pl.BlockSpec((tm, KVW - KL), lambda i: (i % nsb, 0)),
                  pl.BlockSpec((tm, KVW - KL), lambda i: (i % nsb, 0))],
        out_specs=[pl.BlockSpec((tm, QL), lambda i: (i, 0)), pl.BlockSpec((tm, KL), lambda i: (i, 0)),
                   pl.BlockSpec((tm, KVW - KL), lambda i: (i, 0))],
        out_shape=[jax.ShapeDtypeStruct((T, QL), bf16), jax.ShapeDtypeStruct((T, KL), bf16),
                   jax.ShapeDtypeStruct((T, KVW - KL), bf16)],
        compiler_params=_cparams(1), name="mla_down",
    )(h, wq_a_b, wkv_a_b, q_norm_w.reshape(1, QL), kv_norm_w.reshape(1, KL), cos_k, sin_k)


def _mla_attn_kernel(qn_ref, kvc_ref, kpe_ref, wq_ref, wkv_ref, cos_ref, sin_ref, o_ref, q_s, k_s, v_s, *, tq, c_exp):
    S = qn_ref.shape[0]
    nt = (((1,), (1,)), ((), ()))
    q = jnp.dot(qn_ref[...], wq_ref[...], preferred_element_type=f32)
    q_s[...] = _rope(q, cos_ref[...], sin_ref[...]).astype(bf16)
    kv = jnp.dot(kvc_ref[...], wkv_ref[...], preferred_element_type=f32)
    k_s[:, :QK_NOPE] = kv[:, :QK_NOPE].astype(bf16)
    k_s[:, QK_NOPE:] = kpe_ref[...]
    v_s[...] = kv[:, QK_NOPE:].astype(bf16)
    rows = lax.broadcasted_iota(jnp.int32, (tq, tq), 0)
    cols = lax.broadcasted_iota(jnp.int32, (tq, tq), 1)
    for i in range(S // tq):
        lo = i * tq
        qi = q_s[lo:lo + tq, :]
        s_d = lax.dot_general(qi, k_s[lo:lo + tq, :], nt, preferred_element_type=f32)
        s_d = jnp.where(rows >= cols, s_d, -jnp.inf)
        m = jnp.max(s_d, axis=-1, keepdims=True)
        if i > 0:
            s_o = lax.dot_general(qi, k_s[0:lo, :], nt, preferred_element_type=f32)
            m = jnp.maximum(m, jnp.max(s_o, axis=-1, keepdims=True))
        p_d = jnp.exp2((s_d - m) * c_exp)
        l = jnp.sum(p_d, axis=-1, keepdims=True)
        acc = jnp.dot(p_d.astype(bf16), v_s[lo:lo + tq, :], preferred_element_type=f32)
        if i > 0:
            p_o = jnp.exp2((s_o - m) * c_exp)
            l = l + jnp.sum(p_o, axis=-1, keepdims=True)
            acc = acc + jnp.dot(p_o.astype(bf16), v_s[0:lo, :], preferred_element_type=f32)
        o_ref[lo:lo + tq, :] = (acc / l).astype(o_ref.dtype)


def mla_attention(qn, kvc, kpe, wq_heads, wkv_heads, cos_q, sin_q, scale):
    B, S, QL = qn.shape
    KL = kvc.shape[2]
    H, _, W = wq_heads.shape
    tq = min(512, S)
    per_b = lambda w: pl.BlockSpec((None, S, w), lambda b, h: (b, 0, 0))
    per_h = lambda r, w: pl.BlockSpec((None, r, w), lambda b, h: (h, 0, 0))
    table = pl.BlockSpec((S, W), lambda b, h: (0, 0))
    return pl.pallas_call(
        functools.partial(_mla_attn_kernel, tq=tq, c_exp=scale * math.log2(math.e)), grid=(B, H),
        in_specs=[per_b(QL), per_b(KL), per_b(LANES), per_h(QL, W), per_h(KL, QK_NOPE + V_DIM), table, table],
        out_specs=pl.BlockSpec((None, S, V_DIM), lambda b, h: (b, 0, h)),
        out_shape=jax.ShapeDtypeStruct((B, S, H * V_DIM), bf16),
        scratch_shapes=[pltpu.VMEM((S, W), bf16), pltpu.VMEM((S, W), bf16), pltpu.VMEM((S, V_DIM), bf16)],
        compiler_params=_cparams(2), name="mla_attention",
    )(qn, kvc, kpe, wq_heads, wkv_heads, cos_q, sin_q)


def _rope_tables(S, width, rope0):
    half = QK_ROPE // 2
    inv_freq = ROPE_THETA ** (-jnp.arange(half, dtype=f32) * 2.0 / QK_ROPE)
    ang = jnp.arange(S).astype(f32)[:, None] * inv_freq[None, :]
    cos = jnp.repeat(jnp.cos(ang), 2, axis=1)
    sin = jnp.repeat(jnp.sin(ang), 2, axis=1) * jnp.tile(jnp.array([-1.0, 1.0], f32), half)[None, :]
    cos_t = jnp.ones((S, width), f32).at[:, rope0:rope0 + QK_ROPE].set(cos)
    sin_t = jnp.zeros((S, width), f32).at[:, rope0:rope0 + QK_ROPE].set(sin)
    return cos_t, sin_t


def mla_mixer(h, wq_a, q_norm_w, wq_b, wkv_a, kv_norm_w, wkv_b, wo, B, S):
    T, D = h.shape
    H = MLA_HEADS
    KL = kv_norm_w.shape[0]
    QL = wq_a.shape[1]
    W = 2 * LANES
    wkv_a_b = jnp.zeros((D, KL + LANES), f32).at[:, :KL + QK_ROPE].set(wkv_a).astype(bf16)
    wq_heads = jnp.zeros((H, QL, W), f32).at[:, :, :QK_NOPE + QK_ROPE].set(
        wq_b.reshape(QL, H, QK_NOPE + QK_ROPE).transpose(1, 0, 2)).astype(bf16)
    wkv_heads = wkv_b.reshape(KL, H, QK_NOPE + V_DIM).transpose(1, 0, 2).astype(bf16)
    cos_k, sin_k = _rope_tables(S, LANES, 0)
    cos_q, sin_q = _rope_tables(S, W, QK_NOPE)
    qn, kvc, kpe = mla_down(h, wq_a.astype(bf16), wkv_a_b, q_norm_w, kv_norm_w, cos_k, sin_k, S)
    o = mla_attention(qn.reshape(B, S, QL), kvc.reshape(B, S, KL), kpe.reshape(B, S, LANES), wq_heads, wkv_heads,
                      cos_q, sin_q, (QK_NOPE + QK_ROPE) ** -0.5)
    return matmul([o.reshape(T, H * V_DIM)], [wo], f32, 1024, 512, "attn_out_proj")


def ssd_pool_mixer(h, in_proj, conv_w, conv_b, dt_bias, a_log, d_skip, norm_w, pool_w, pool_scale, out_proj, B, S):
    T, D = h.shape
    DS = norm_w.shape[0]
    NH = dt_bias.shape[0]
    GN = SSM_GROUPS * SSM_STATE
    o1 = DS + DS + 2 * GN
    o2 = o1 + NH
    assert o1 % LANES == 0
    zx = matmul([h], [in_proj], f32, 1024, 512, "in_proj_zx", N=o1).reshape(B, S, o1)
    dt_raw = matmul([h], [in_proj], f32, 1024, LANES, "in_proj_dt", N=LANES,
                    w_blocks=[(0, o1 // LANES)]).reshape(B, S, LANES)
    u = matmul([h], [in_proj[:, o2:]], f32, 1024, 512, "in_proj_u").reshape(B, S, -1)
    xs = conv_silu(zx, DS, DS, conv_w, conv_b, 0, f32)
    bc = conv_silu(zx, 2 * DS, 2 * GN, conv_w, conv_b, DS, bf16)
    pad = lambda v: jnp.zeros((1, LANES), f32).at[0, :NH].set(v)
    a_row = pad(-jnp.exp(a_log))
    dskip_row = jnp.repeat(d_skip, SSM_HEAD_DIM).reshape(1, DS)
    y = ssd_mixer(xs, zx, dt_raw, bc, a_row, pad(dt_bias), dskip_row, norm_w)
    y_pool = pool_mixer(u, pool_w.astype(bf16), pool_scale)
    assert y_pool.shape[2] == DS
    return matmul([y.reshape(T, DS), y_pool.reshape(T, DS)], [out_proj, out_proj], f32, 1024, 256, "mixer_out_proj",
                  w_blocks=[(0, 0), (1, 0)])


def kernel(x, c, ada_w, ada_b, ln_w, ln_b, in_proj, conv_w, conv_b, dt_bias, a_log, d_skip, ssd_norm_w, pool_w, pool_scale, out_proj, wq_a, q_norm_w, wq_b, wkv_a, kv_norm_w, wkv_b, wo, router_w, router_b, w_gu, b_gu, w_down, b_down):
    B, S, D = x.shape
    T = B * S
    mod = ada_modulation(c, ada_w, ada_b)
    m = lambda l, k: mod[l, :, k].reshape(B, 1, D)
    h = modulate(x, m(0, 1), m(0, 0))
    for i in range(DEPTH):
        j = i // 2
        hf = h.reshape(T, D)
        if i % 2 == 0:
            y = ssd_pool_mixer(hf, in_proj[j], conv_w[j], conv_b[j], dt_bias[j], a_log[j], d_skip[j],
                               ssd_norm_w[j], pool_w[j], pool_scale[j], out_proj[j], B, S)
        else:
            y = mla_mixer(hf, wq_a[j], q_norm_w[j], wq_b[j], wkv_a[j], kv_norm_w[j], wkv_b[j], wo[j], B, S)
        x, hp, logits = post_ln(x, [y.reshape(B, S, D)], m(i, 2), ln_w[i, 0], ln_b[i, 0], m(i, 4), m(i, 3),
                                router_w[i], router_b[i])
        yp = moe_ffn(i, hp.reshape(T, D // 2), logits.reshape(T, LANES), w_gu, b_gu, w_down, b_down)
        if i + 1 < DEPTH:
            x, h = post_ln(x, [yp], m(i, 5), ln_w[i, 1], ln_b[i, 1], m(i + 1, 1), m(i + 1, 0), packed_y=True)
        else:
            (x,) = post_ln(x, [yp], m(i, 5), ln_w[i, 1], ln_b[i, 1], packed_y=True)
    return x
```
